```python
import math
import jax, jax.numpy as jnp
from jax import lax
import numpy as np

D_MODEL = 1024
BATCH = 2
SEQ = 8192
DEPTH = 2

GRID_W = 64
CTX_LEN = 256
EPS = 1e-6

GDN_HEADS = 4
GDN_DK = 128
GDN_DV = 128
GDN_CONV = 5
GDN_CHUNK = 64
CMLP_GROUPS = 4
CMLP_CH = 128
CMLP_CHUNK = 128
DA_HEADS = 4
DA_DH = 64
DA_DV = 2 * DA_DH
DA_QBLOCK = 128
ROPE_BASE = 10000.0
PEER_HEADS = 8
PEER_NKEYS = 128
PEER_EXPERTS = PEER_NKEYS * PEER_NKEYS
PEER_QDIM = 256
PEER_TOPK = 16
PEER_TBLOCK = 128

MIX_A = GDN_HEADS * GDN_DV
MIX_B = CMLP_GROUPS * CMLP_CH
MIX_C = DA_HEADS * DA_DV
GDN_QKV = GDN_HEADS * (2 * GDN_DK + GDN_DV)

IN_SIZES = (
    GDN_QKV,
    GDN_HEADS * GDN_DV,
    4 * GDN_HEADS,
    2 * MIX_B,
    DA_HEADS * 2 * DA_DH,
    DA_HEADS * 2 * DA_DH,
    DA_HEADS * DA_DV,
    3 * D_MODEL,
)

kernel_name = 'hybrid_gdn_chunkmlp_diffattn_peer'


def _split(t, sizes):
    out, start = [], 0
    for s in sizes:
        out.append(t[..., start:start + s])
        start += s
    return out


def _rmsnorm(x, g):
    xf = x.astype(jnp.float32)
    y = xf * lax.rsqrt(jnp.mean(xf * xf, axis=-1, keepdims=True) + EPS)
    return (y * g.astype(jnp.float32)).astype(x.dtype)


def _l2norm(x):
    xf = x.astype(jnp.float32)
    return xf * lax.rsqrt(jnp.sum(xf * xf, axis=-1, keepdims=True) + EPS)


def _modulate(h, shift, scale):
    return h * (1.0 + scale) + shift


def _short_conv(x, w):
    k = w.shape[0]
    return lax.conv_general_dilated(x, w[:, None, :].astype(x.dtype), window_strides=(1,),
                                    padding=[(k // 2, k // 2)], dimension_numbers=('NWC', 'WIO', 'NWC'),
                                    feature_group_count=x.shape[-1])


def _gdn_prep(qkv, ba, conv_w, a_log, dt_bias):
    b, l, _ = qkv.shape
    qkv = jax.nn.silu(_short_conv(qkv, conv_w))
    q, k, v = _split(qkv, (GDN_HEADS * GDN_DK, GDN_HEADS * GDN_DK, GDN_HEADS * GDN_DV))
    q = _l2norm(q.reshape(b, l, GDN_HEADS, GDN_DK)) * GDN_DK ** -0.5
    k = _l2norm(k.reshape(b, l, GDN_HEADS, GDN_DK))
    v = v.reshape(b, l, GDN_HEADS, GDN_DV).astype(jnp.float32)
    ba = ba.astype(jnp.float32).reshape(b, l, 2, 2, GDN_HEADS)
    beta = jax.nn.sigmoid(ba[:, :, 0])
    g = -jnp.exp(a_log.astype(jnp.float32)) * jax.nn.softplus(ba[:, :, 1] + dt_bias.astype(jnp.float32))
    return q, k, v, g, beta


def _gated_delta_chunked(q, k, v, g, beta, s0):
    b, l, h, dk = q.shape
    dv = v.shape[-1]
    c = GDN_CHUNK
    n = l // c

    def chunks(t):
        return jnp.moveaxis(t.reshape(b, n, c, h, *t.shape[3:]), 3, 1)

    qc, kc, vc = chunks(q), chunks(k), chunks(v)
    gc = jnp.cumsum(chunks(g), axis=-1)
    bc = chunks(beta)
    incl = jnp.tril(jnp.ones((c, c), dtype=bool))
    strict = jnp.tril(jnp.ones((c, c), dtype=bool), -1)
    gdiff = gc[..., :, None] - gc[..., None, :]
    decay = jnp.where(incl, jnp.exp(jnp.where(incl, gdiff, 0.0)), 0.0)
    kk = jnp.einsum('bhncd,bhnsd->bhncs', kc, kc)
    lower = jnp.where(strict, kk * decay * bc[..., :, None], 0.0)
    eye = jnp.eye(c, dtype=jnp.float32)
    t_inv = lax.linalg.triangular_solve(eye + lower, jnp.broadcast_to(eye, lower.shape),
                                        left_side=True, lower=True, unit_diagonal=True)
    u = jnp.einsum('bhncs,bhnsd->bhncd', t_inv, vc * bc[..., None])
    w = jnp.einsum('bhncs,bhnsd->bhncd', t_inv, kc * (bc * jnp.exp(gc))[..., None])
    qk = jnp.einsum('bhncd,bhnsd->bhncs', qc, kc) * decay
    q_dec = qc * jnp.exp(gc)[..., None]
    k_dec = kc * jnp.exp(gc[..., -1:] - gc)[..., None]
    g_last = jnp.exp(gc[..., -1])

    def step(state, xs):
        u_i, w_i, qk_i, qd_i, kd_i, gl_i = xs
        v_new = u_i - jnp.einsum('bhcd,bhde->bhce', w_i, state)
        o_i = jnp.einsum('bhcd,bhde->bhce', qd_i, state) + jnp.einsum('bhcs,bhse->bhce', qk_i, v_new)
        state = state * gl_i[..., None, None] + jnp.einsum('bhcd,bhce->bhde', kd_i, v_new)
        return state, o_i

    xs = tuple(jnp.moveaxis(t, 2, 0) for t in (u, w, qk, q_dec, k_dec, g_last))
    s_fin, o = lax.scan(step, s0, xs)
    o = jnp.transpose(o, (1, 0, 3, 2, 4)).reshape(b, l, h, dv)
    return o, s_fin


def _gdn_dir(prep, d, s0):
    q, k, v, g, beta = prep
    g, beta = g[:, :, d], beta[:, :, d]
    if d == 1:
        q, k, v, g, beta = (jnp.flip(t, axis=1) for t in (q, k, v, g, beta))
    o, s = _gated_delta_chunked(q, k, v, g, beta, s0)
    if d == 1:
        o = jnp.flip(o, axis=1)
    return o, s


def _gdn_out(o, z, norm_g):
    b, l = z.shape[:2]
    zz = z.reshape(b, l, GDN_HEADS, GDN_DV).astype(jnp.float32)
    y = _rmsnorm(o, norm_g) * jax.nn.silu(zz)
    return y.reshape(b, l, MIX_A).astype(z.dtype)


def _chunk_mlp(uv, v_norm_g, w_s, b_s):
    b, l, _ = uv.shape
    u, v = jnp.split(jax.nn.gelu(uv), 2, axis=-1)
    v = _rmsnorm(v, v_norm_g).reshape(b, l // CMLP_CHUNK, CMLP_CHUNK, CMLP_GROUPS, CMLP_CH)
    sv = jnp.einsum('gpq,bnqgc->bnpgc', w_s, v) + b_s.T[:, :, None]
    return u * sv.reshape(b, l, MIX_B)


def _axial_rope(rows):
    r, col = jnp.meshgrid(jnp.arange(rows), jnp.arange(GRID_W), indexing='ij')
    n_freq = DA_DH // 4
    inv = ROPE_BASE ** (-jnp.arange(n_freq, dtype=jnp.float32) / n_freq)
    ang = jnp.concatenate([r.reshape(-1, 1).astype(jnp.float32) * inv,
                           col.reshape(-1, 1).astype(jnp.float32) * inv], axis=-1)
    return jnp.cos(ang), jnp.sin(ang)


def _rope(x, cos, sin):
    half = DA_DH // 2
    x1, x2 = x[..., :half], x[..., half:]
    cs, sn = cos[:, None, None, :], sin[:, None, None, :]
    return jnp.concatenate([x1 * cs - x2 * sn, x1 * sn + x2 * cs], axis=-1).astype(x.dtype)


def _da_qk(t, g, rope):
    b, l, _ = t.shape
    t = _rmsnorm(t.reshape(b, l, DA_HEADS, 2, DA_DH), g)
    if rope is not None:
        t = _rope(t, *rope)
    return jnp.transpose(t, (0, 2, 3, 1, 4))


def _da_v(t):
    b, l, _ = t.shape
    return jnp.transpose(t.reshape(b, l, DA_HEADS, DA_DV), (0, 2, 1, 3))


def _da_attend(q, k_all, v_all, lam):
    s = jnp.einsum('bhiqd,bhikd->bhiqk', q, k_all, preferred_element_type=jnp.float32) * DA_DH ** -0.5
    p = jax.nn.softmax(s, axis=-1)
    a = p[:, :, 0] - lam * p[:, :, 1]
    return jnp.einsum('bhqk,bhkd->bhqd', a.astype(v_all.dtype), v_all)


def _da_out(o, subln_g, lam_init):
    b, h, l, dv = o.shape
    o = _rmsnorm(o, subln_g) * (1.0 - lam_init)
    return jnp.transpose(o, (0, 2, 1, 3)).reshape(b, l, MIX_C)


def _merge(oa, ob, oc, gates, wa, wb, wc, wo):
    ga, gb, gc = jnp.split(jax.nn.sigmoid(gates), 3, axis=-1)
    y = ga * (oa @ wa) + gb * (ob @ wb) + gc * (oc @ wc)
    return y @ wo


def _peer(h, w_q, subkeys, u_tab, v_tab):
    t, d = h.shape
    q = (h @ w_q).reshape(t, PEER_HEADS, 2, PEER_QDIM // 2)
    s = jnp.einsum('thid,hind->thin', q, subkeys, preferred_element_type=jnp.float32)
    s_top, i_top = lax.top_k(s, PEER_TOPK)
    kk = PEER_TOPK * PEER_TOPK
    cand = (s_top[:, :, 0, :, None] + s_top[:, :, 1, None, :]).reshape(t, PEER_HEADS, kk)
    cand_idx = (i_top[:, :, 0, :, None] * PEER_NKEYS + i_top[:, :, 1, None, :]).reshape(t, PEER_HEADS, kk)
    best, pos = lax.top_k(cand, PEER_TOPK)
    idx = jnp.take_along_axis(cand_idx, pos, axis=-1)
    gate = jax.nn.softmax(best, axis=-1)
    nb = t // PEER_TBLOCK

    def block(args):
        hb, ib, gb = args
        act = jax.nn.gelu(jnp.einsum('td,thkd->thk', hb, u_tab[ib], preferred_element_type=jnp.float32))
        return jnp.einsum('thk,thkd->td', (gb * act).astype(v_tab.dtype), v_tab[ib])

    out = lax.map(block, (h.reshape(nb, PEER_TBLOCK, d),
                          idx.reshape(nb, PEER_TBLOCK, PEER_HEADS, PEER_TOPK),
                          gate.reshape(nb, PEER_TBLOCK, PEER_HEADS, PEER_TOPK)))
    return out.reshape(t, d)


def setup_inputs(seed: int = 0) -> dict:
    key = jax.random.key(seed)
    ks = jax.random.split(key, 28)
    D = D_MODEL
    f32 = jnp.float32

    def nrm(k, shape, std):
        return jax.random.normal(k, shape, f32) * std

    def gain(k, shape):
        return 1.0 + 0.02 * jax.random.normal(k, shape, f32)

    in_cols = sum(IN_SIZES)
    dt = jnp.exp(jax.random.uniform(ks[10], (DEPTH, 2, GDN_HEADS), dtype=f32,
                                    minval=math.log(1e-3), maxval=math.log(1e-1)))
    return {
        'x': nrm(ks[0], (BATCH, SEQ, D), 1.0),
        'c': nrm(ks[1], (BATCH, D), 1.0),
        'ctx': nrm(ks[2], (BATCH, CTX_LEN, D), 1.0),
        'c_ctx': nrm(ks[3], (D,), 1.0),
        'w_mod': nrm(ks[4], (DEPTH, D, 6 * D), 0.5 * D ** -0.5),
        'b_mod': nrm(ks[5], (DEPTH, 6 * D), 0.02),
        'norm1_g': gain(ks[6], (DEPTH, D)),
        'w_in': nrm(ks[7], (DEPTH, D, in_cols), D ** -0.5),
        'gdn_conv_w': nrm(ks[8], (DEPTH, GDN_CONV, GDN_QKV), GDN_CONV ** -0.5),
        'gdn_a_log': jnp.log(jax.random.uniform(ks[9], (DEPTH, 2, GDN_HEADS), dtype=f32, minval=1.0, maxval=16.0)),
        'gdn_dt_bias': dt + jnp.log(-jnp.expm1(-dt)),
        'gdn_out_norm_g': gain(ks[11], (DEPTH, GDN_DV)),
        'cmlp_v_norm_g': gain(ks[12], (DEPTH, MIX_B)),
        'cmlp_w_s': nrm(ks[13], (DEPTH, CMLP_GROUPS, CMLP_CHUNK, CMLP_CHUNK), CMLP_CHUNK ** -0.5),
        'cmlp_b_s': gain(ks[14], (DEPTH, CMLP_GROUPS, CMLP_CHUNK)),
        'da_q_norm_g': gain(ks[15], (DEPTH, DA_DH)),
        'da_k_norm_g': gain(ks[16], (DEPTH, DA_DH)),
        'da_lambda': nrm(ks[17], (DEPTH, 4, DA_DH), 0.1),
        'da_subln_g': gain(ks[18], (DEPTH, DA_DV)),
        'w_branch_a': nrm(ks[19], (DEPTH, MIX_A, D), MIX_A ** -0.5),
        'w_branch_b': nrm(ks[20], (DEPTH, MIX_B, D), MIX_B ** -0.5),
        'w_branch_c': nrm(ks[21], (DEPTH, MIX_C, D), MIX_C ** -0.5),
        'w_out': nrm(ks[22], (DEPTH, D, D), D ** -0.5),
        'norm2_g': gain(ks[23], (DEPTH, D)),
        'peer_w_q': nrm(ks[24], (DEPTH, D, PEER_HEADS * PEER_QDIM), D ** -0.5),
        'peer_subkeys': nrm(ks[25], (DEPTH, PEER_HEADS, 2, PEER_NKEYS, PEER_QDIM // 2), (PEER_QDIM // 2) ** -0.5),
        'peer_u': nrm(ks[26], (DEPTH, PEER_EXPERTS, D), D ** -0.5),
        'peer_v': nrm(ks[27], (DEPTH, PEER_EXPERTS, D), 1.0),
    }


def reference(x, c, ctx, c_ctx, w_mod, b_mod, norm1_g, w_in, gdn_conv_w, gdn_a_log, gdn_dt_bias,
              gdn_out_norm_g, cmlp_v_norm_g, cmlp_w_s, cmlp_b_s, da_q_norm_g, da_k_norm_g, da_lambda,
              da_subln_g, w_branch_a, w_branch_b, w_branch_c, w_out, norm2_g, peer_w_q, peer_subkeys,
              peer_u, peer_v):
    b, L, d = x.shape
    lc = ctx.shape[1]
    rows = L // GRID_W
    rope = _axial_rope(rows)
    silu_c = jax.nn.silu(c)[:, None, :]
    silu_cc = jax.nn.silu(c_ctx)[None, None, :]
    n_qb = L // DA_QBLOCK
    for l in range(DEPTH):
        last = l == DEPTH - 1
        lam_init = 0.8 - 0.6 * math.exp(-0.3 * l)
        mx = jnp.split(silu_c @ w_mod[l] + b_mod[l], 6, axis=-1)
        mc = jnp.split(silu_cc @ w_mod[l] + b_mod[l], 6, axis=-1)

        hx = _modulate(_rmsnorm(x, norm1_g[l]), mx[0], mx[1])
        hc = _modulate(_rmsnorm(ctx, norm1_g[l]), mc[0], mc[1])
        px = _split(hx @ w_in[l], IN_SIZES)
        pc = _split(hc @ w_in[l], IN_SIZES)

        prep_x = _gdn_prep(px[0], px[2], gdn_conv_w[l], gdn_a_log[l], gdn_dt_bias[l])
        prep_c = _gdn_prep(pc[0], pc[2], gdn_conv_w[l], gdn_a_log[l], gdn_dt_bias[l])
        s_zero = jnp.zeros((b, GDN_HEADS, GDN_DK, GDN_DV), jnp.float32)
        oa_x, oa_c = 0.0, 0.0
        for dirn in range(2):
            o_c, s_c = _gdn_dir(prep_c, dirn, s_zero)
            o_x, _ = _gdn_dir(prep_x, dirn, s_c)
            oa_x = oa_x + o_x
            if not last:
                oa_c = oa_c + o_c
        ya_x = _gdn_out(oa_x, px[1], gdn_out_norm_g[l])

        yb_x = _chunk_mlp(px[3], cmlp_v_norm_g[l], cmlp_w_s[l], cmlp_b_s[l])

        lam_p = da_lambda[l].astype(jnp.float32)
        lam = jnp.exp(jnp.sum(lam_p[0] * lam_p[1])) - jnp.exp(jnp.sum(lam_p[2] * lam_p[3])) + lam_init
        qx = _da_qk(px[4], da_q_norm_g[l], rope)
        kx = _da_qk(px[5], da_k_norm_g[l], rope)
        vx = _da_v(px[6])
        kc = _da_qk(pc[5], da_k_norm_g[l], None)
        vc = _da_v(pc[6])
        k_all = jnp.concatenate([kc, kx], axis=3)
        v_all = jnp.concatenate([vc, vx], axis=2)
        q_blocks = jnp.moveaxis(qx.reshape(b, DA_HEADS, 2, n_qb, DA_QBLOCK, DA_DH), 3, 0)
        o_blocks = lax.map(lambda qb: _da_attend(qb, k_all, v_all, lam), q_blocks)
        ox = jnp.moveaxis(o_blocks, 0, 2).reshape(b, DA_HEADS, L, DA_DV)
        yc_x = _da_out(ox, da_subln_g[l], lam_init)

        mix_x = _merge(ya_x, yb_x, yc_x, px[7], w_branch_a[l], w_branch_b[l], w_branch_c[l], w_out[l])
        if not last:
            ya_c = _gdn_out(oa_c, pc[1], gdn_out_norm_g[l])
            yb_c = _chunk_mlp(pc[3], cmlp_v_norm_g[l], cmlp_w_s[l], cmlp_b_s[l])
            qc = _da_qk(pc[4], da_q_norm_g[l], None)
            yc_c = _da_out(_da_attend(qc, kc, vc, lam), da_subln_g[l], lam_init)
            mix_c = _merge(ya_c, yb_c, yc_c, pc[7], w_branch_a[l], w_branch_b[l], w_branch_c[l], w_out[l])
            ctx = ctx + mc[2] * mix_c
        x = x + mx[2] * mix_x

        hx = _modulate(_rmsnorm(x, norm2_g[l]), mx[3], mx[4])
        if last:
            fx = _peer(hx.reshape(b * L, d), peer_w_q[l], peer_subkeys[l], peer_u[l], peer_v[l]).reshape(b, L, d)
        else:
            hc = _modulate(_rmsnorm(ctx, norm2_g[l]), mc[3], mc[4])
            f = _peer(jnp.concatenate([hx.reshape(b * L, d), hc.reshape(b * lc, d)], axis=0),
                      peer_w_q[l], peer_subkeys[l], peer_u[l], peer_v[l])
            fx = f[:b * L].reshape(b, L, d)
            ctx = ctx + mc[5] * f[b * L:].reshape(b, lc, d)
        x = x + mx[5] * fx
    return x
```

```python
import functools
import math

import jax
import jax.numpy as jnp
from jax import lax
from jax.experimental import pallas as pl
from jax.experimental.pallas import tpu as pltpu

F32 = jnp.float32
BF16 = jnp.bfloat16

D = 1024
EPS = 1e-6
GRID_W = 64
GDN_H, GDN_DK, GDN_DV, GDN_CONV, GDN_C = 4, 128, 128, 5, 64
CMLP_G, CMLP_CH, CMLP_CHUNK = 4, 128, 128
DA_H, DA_DH, DA_DV = 4, 64, 128
ROPE_BASE = 10000.0
PEER_H, PEER_NK, PEER_QD, PEER_K = 8, 128, 256, 16
PEER_E = PEER_NK * PEER_NK
HALF = GDN_H * GDN_DV

COL_GATES, COL_QKV, COL_Z, COL_UV, COL_Q, COL_K, COL_V, COL_BA = 0, 3072, 4608, 5120, 6144, 6656, 7168, 7680
N_PROJ = 8192

NEG = -1e30
VMEM_LIMIT = 56 * 1024 * 1024


def _cparams(sem):
    return pltpu.CompilerParams(dimension_semantics=sem, vmem_limit_bytes=VMEM_LIMIT)


def _silu(x):
    return x * (1.0 / (1.0 + jnp.exp(-x)))


def _sigmoid(x):
    return 1.0 / (1.0 + jnp.exp(-x))


def _gelu(x):
    return 0.5 * x * (1.0 + jnp.tanh(0.7978845608028654 * (x + 0.044715 * (x * x * x))))


def _dot(a, b):
    return jnp.dot(a, b, preferred_element_type=F32)


def _dot_nt(a, b):
    return lax.dot_general(a, b, (((1,), (1,)), ((), ())), preferred_element_type=F32)


def _dot_tn(a, b):
    return lax.dot_general(a, b, (((0,), (0,)), ((), ())), preferred_element_type=F32)


def _mod_kernel(c_ref, w_ref, b_ref, o_ref):
    o_ref[...] = jnp.dot(_silu(c_ref[...]), w_ref[...], preferred_element_type=F32,
                         precision=lax.Precision.HIGHEST) + b_ref[...]


def _mod_call(cin, w_mod, b_mod):
    depth = w_mod.shape[0]
    tn = 1024
    return pl.pallas_call(
        _mod_kernel,
        grid=(depth, 6 * D // tn),
        in_specs=[pl.BlockSpec((8, D), lambda l, j: (0, 0)),
                  pl.BlockSpec((None, D, tn), lambda l, j: (l, 0, j)),
                  pl.BlockSpec((None, 1, tn), lambda l, j: (l, 0, j))],
        out_specs=pl.BlockSpec((None, 8, tn), lambda l, j: (l, 0, j)),
        out_shape=jax.ShapeDtypeStruct((depth, 8, 6 * D), F32),
        compiler_params=_cparams(("parallel", "parallel")),
        name="mod",
    )(cin, w_mod, b_mod.reshape(depth, 1, 6 * D))


def _inproj_kernel(x_ref, mod_ref, g_ref, w_ref, o_ref, hn_ref):
    @pl.when(pl.program_id(1) == 0)
    def _():
        x = x_ref[...]
        y = x * lax.rsqrt(jnp.mean(x * x, axis=-1, keepdims=True) + EPS) * g_ref[...]
        hn_ref[...] = (y * (1.0 + mod_ref[:, D:2 * D]) + mod_ref[:, 0:D]).astype(BF16)

    o_ref[...] = _dot(hn_ref[...], w_ref[...])


def _inproj_call(xs, mods, g, w, grp, tm=512, tn=512):
    r = xs.shape[0]
    return pl.pallas_call(
        _inproj_kernel,
        grid=(r // tm, N_PROJ // tn),
        in_specs=[pl.BlockSpec((tm, D), lambda i, j: (i, 0)),
                  pl.BlockSpec((None, 1, 6 * D), lambda i, j: (grp(i * tm), 0, 0)),
                  pl.BlockSpec((1, D), lambda i, j: (0, 0)),
                  pl.BlockSpec((D, tn), lambda i, j: (0, j))],
        out_specs=pl.BlockSpec((tm, tn), lambda i, j: (i, j)),
        out_shape=jax.ShapeDtypeStruct((r, N_PROJ), F32),
        scratch_shapes=[pltpu.VMEM((tm, D), BF16)],
        compiler_params=_cparams(("parallel", "arbitrary")),
        name="inproj",
    )(xs, mods, g, w)


def _gdn_prep_kernel(x_ref, prev_ref, next_ref, ba_ref, cw_ref, gp_ref,
                     q_ref, k_ref, v_ref, gb_ref, ext_ref, *, tp, bl, l, lc):
    i = pl.program_id(0)
    r0 = i * tp
    r1 = r0 + tp
    is_start = jnp.where(r0 < bl, r0 % l == 0, (r0 - bl) % lc == 0)
    is_end = jnp.where(r1 <= bl, r1 % l == 0, (r1 - bl) % lc == 0)
    ext_ref[0:8, :] = jnp.where(is_start, 0.0, prev_ref[...])
    ext_ref[8:8 + tp, :] = x_ref[...]
    ext_ref[8 + tp:16 + tp, :] = jnp.where(is_end, 0.0, next_ref[...])
    y = cw_ref[0:1, :] * ext_ref[6:6 + tp, :]
    for j in range(1, GDN_CONV):
        y = y + cw_ref[j:j + 1, :] * ext_ref[6 + j:6 + j + tp, :]
    y = _silu(y)
    for h in range(GDN_H):
        qh = y[:, h * GDN_DK:(h + 1) * GDN_DK]
        q_ref[:, h * GDN_DK:(h + 1) * GDN_DK] = qh * (
            lax.rsqrt(jnp.sum(qh * qh, axis=-1, keepdims=True) + EPS) * (GDN_DK ** -0.5))
        kh = y[:, HALF + h * GDN_DK:HALF + (h + 1) * GDN_DK]
        k_ref[:, h * GDN_DK:(h + 1) * GDN_DK] = kh * lax.rsqrt(jnp.sum(kh * kh, axis=-1, keepdims=True) + EPS)
    v_ref[...] = y[:, 2 * HALF:3 * HALF]
    ba = ba_ref[...]
    beta = _sigmoid(ba)
    t = ba + gp_ref[1:2, :]
    sp = jnp.maximum(t, 0.0) + jnp.log(1.0 + jnp.exp(-jnp.abs(t)))
    g = -jnp.exp(gp_ref[0:1, :]) * sp
    lane = lax.broadcasted_iota(jnp.int32, ba.shape, 1)
    gb_ref[...] = jnp.where(lane < 2 * GDN_H, beta, g)


def _gdn_prep_call(proj, conv_w, gparams, bl, l, lc, tp=256):
    r = proj.shape[0]
    nb8 = r // 8
    qkv_blk = COL_QKV // (3 * HALF)
    kern = functools.partial(_gdn_prep_kernel, tp=tp, bl=bl, l=l, lc=lc)
    out_sd = jax.ShapeDtypeStruct((r, HALF), F32)
    return pl.pallas_call(
        kern,
        grid=(r // tp,),
        in_specs=[pl.BlockSpec((tp, 3 * HALF), lambda i: (i, qkv_blk)),
                  pl.BlockSpec((8, 3 * HALF), lambda i: (jnp.maximum(i * (tp // 8) - 1, 0), qkv_blk)),
                  pl.BlockSpec((8, 3 * HALF), lambda i: (jnp.minimum((i + 1) * (tp // 8), nb8 - 1), qkv_blk)),
                  pl.BlockSpec((tp, 128), lambda i: (i, COL_BA // 128)),
                  pl.BlockSpec((GDN_CONV, 3 * HALF), lambda i: (0, 0)),
                  pl.BlockSpec((2, 128), lambda i: (0, 0))],
        out_specs=[pl.BlockSpec((tp, HALF), lambda i: (i, 0)),
                   pl.BlockSpec((tp, HALF), lambda i: (i, 0)),
                   pl.BlockSpec((tp, HALF), lambda i: (i, 0)),
                   pl.BlockSpec((tp, 128), lambda i: (i, 0))],
        out_shape=[out_sd, out_sd, out_sd, jax.ShapeDtypeStruct((r, 128), F32)],
        scratch_shapes=[pltpu.VMEM((tp + 16, 3 * HALF), F32)],
        compiler_params=_cparams(("parallel",)),
        name="gdn_prep",
    )(proj, proj, proj, proj, conv_w, gparams)


def _gdn_chain(q, k, v, gb, d, h, s_ref):
    c = GDN_C
    row = lax.broadcasted_iota(jnp.int32, (c, c), 0)
    col = lax.broadcasted_iota(jnp.int32, (c, c), 1)
    if d == 0:
        incl, strict = row >= col, row > col
    else:
        incl, strict = row <= col, row < col
    hp = lax.Precision.HIGHEST
    gc_col = jnp.dot(jnp.where(incl, 1.0, 0.0), gb, preferred_element_type=F32, precision=hp)
    tri_t = jnp.where(row <= col if d == 0 else row >= col, 1.0, 0.0)
    gc_row = lax.dot_general(gb, tri_t, (((0,), (0,)), ((), ())), preferred_element_type=F32,
                             precision=hp)
    cb, cg = d * GDN_H + h, 2 * GDN_H + d * GDN_H + h
    gcol = gc_col[:, cg:cg + 1]
    grow = gc_row[cg:cg + 1, :]
    beta = gb[:, cb:cb + 1]
    last = c - 1 if d == 0 else 0
    glast = gc_col[last:last + 1, cg:cg + 1]

    decay = jnp.where(incl, jnp.exp(jnp.where(incl, gcol - grow, 0.0)), 0.0)
    kb = k.astype(BF16)
    kk = _dot_nt(kb, kb)
    x = jnp.where(strict, -(kk * decay * beta), 0.0)
    eye = jnp.where(row == col, 1.0, 0.0)
    tinv = eye + x
    p = x
    for _ in range(5):
        pb = p.astype(BF16)
        p = _dot(pb, pb)
        tinv = tinv + _dot(tinv.astype(BF16), p.astype(BF16))
    eg = jnp.exp(gcol)
    rhs = jnp.concatenate([v * beta, k * (beta * eg)], axis=1).astype(BF16)
    uw = _dot(tinv.astype(BF16), rhs)
    u, w = uw[:, :GDN_DV], uw[:, GDN_DV:]
    qk = _dot_nt(q.astype(BF16), kb) * decay
    q_dec = q * eg
    k_dec = k * jnp.exp(glast - gcol)
    s = s_ref[d, h]
    sb = s.astype(BF16)
    ws = _dot(jnp.concatenate([w, q_dec], axis=0).astype(BF16), sb)
    v_new = u - ws[:c]
    vb = v_new.astype(BF16)
    o = ws[c:] + _dot(qk.astype(BF16), vb)
    s_ref[d, h] = s * jnp.exp(glast) + _dot_tn(k_dec.astype(BF16), vb)
    return o


def _gdn_scan_kernel(qf, kf, vf, gf, qb, kb, vb, gb, of_ref, ob_ref, s_ref):
    @pl.when(pl.program_id(1) == 0)
    def _():
        s_ref[...] = jnp.zeros_like(s_ref)

    for d, (q_ref, k_ref, v_ref, g_ref, o_ref) in enumerate(((qf, kf, vf, gf, of_ref), (qb, kb, vb, gb, ob_ref))):
        g = g_ref[...]
        for h in range(GDN_H):
            sl = slice(h * GDN_DK, (h + 1) * GDN_DK)
            o_ref[:, sl] = _gdn_chain(q_ref[:, sl], k_ref[:, sl], v_ref[:, sl], g, d, h, s_ref)


def _gdn_scan_call(q, k, v, gbm, b, l, lc):
    r = q.shape[0]
    c = GDN_C
    ncc, ncl = lc // c, l // c
    base = b * l // c

    def fwd(bi, s):
        return jnp.where(s < ncc, base + bi * ncc + s, bi * ncl + s - ncc)

    def bwd(bi, s):
        return jnp.where(s < ncc, base + bi * ncc + (ncc - 1 - s), bi * ncl + (ncl - 1) - (s - ncc))

    def specs(fn):
        return [pl.BlockSpec((c, HALF), lambda bi, s: (fn(bi, s), 0))] * 3 + [
            pl.BlockSpec((c, 128), lambda bi, s: (fn(bi, s), 0))]

    out_sd = jax.ShapeDtypeStruct((r, HALF), F32)
    return pl.pallas_call(
        _gdn_scan_kernel,
        grid=(b, ncc + ncl),
        in_specs=specs(fwd) + specs(bwd),
        out_specs=[pl.BlockSpec((c, HALF), lambda bi, s: (fwd(bi, s), 0)),
                   pl.BlockSpec((c, HALF), lambda bi, s: (bwd(bi, s), 0))],
        out_shape=[out_sd, out_sd],
        scratch_shapes=[pltpu.VMEM((2, GDN_H, GDN_DK, GDN_DV), F32)],
        compiler_params=_cparams(("parallel", "arbitrary")),
        name="gdn_scan",
    )(q, k, v, gbm, q, k, v, gbm)


def _da_prep_kernel(q_ref, k_ref, v_ref, cos_ref, sin_ref, gq_ref, gk_ref, qs_ref, kp_ref, vp_ref):
    cs, sn = cos_ref[...], sin_ref[...]
    lane = lax.broadcasted_iota(jnp.int32, cs.shape, 1)
    lo64 = lane < DA_DH
    lo32 = (lane % DA_DH) < (DA_DH // 2)

    def norm_rope(t, g):
        sq = t * t
        ms0 = jnp.sum(jnp.where(lo64, sq, 0.0), axis=-1, keepdims=True)
        ms1 = jnp.sum(jnp.where(lo64, 0.0, sq), axis=-1, keepdims=True)
        inv = lax.rsqrt(jnp.where(lo64, ms0, ms1) * (1.0 / DA_DH) + EPS)
        t = t * inv * g
        swapped = jnp.where(lo32, pltpu.roll(t, 128 - DA_DH // 2, 1), pltpu.roll(t, DA_DH // 2, 1))
        return t * cs + swapped * sn

    for h in range(DA_H):
        sl = slice(h * 128, (h + 1) * 128)
        qh = norm_rope(q_ref[:, sl], gq_ref[...]) * (DA_DH ** -0.5)
        qs_ref[h, 0] = jnp.where(lo64, qh, 0.0).astype(BF16)
        qs_ref[h, 1] = jnp.where(lo64, 0.0, qh).astype(BF16)
        kp_ref[h] = norm_rope(k_ref[:, sl], gk_ref[...]).astype(BF16)
        vp_ref[h] = v_ref[:, sl].astype(BF16)


def _da_prep_call(proj, cos_t, sin_t, gq, gk, bl, l, tp=256):
    r = proj.shape[0]
    ntl = l // tp

    def pos(i):
        return jnp.where(i * tp < bl, i % ntl, ntl)

    return pl.pallas_call(
        _da_prep_kernel,
        grid=(r // tp,),
        in_specs=[pl.BlockSpec((tp, HALF), lambda i: (i, COL_Q // HALF)),
                  pl.BlockSpec((tp, HALF), lambda i: (i, COL_K // HALF)),
                  pl.BlockSpec((tp, HALF), lambda i: (i, COL_V // HALF)),
                  pl.BlockSpec((tp, 128), lambda i: (pos(i), 0)),
                  pl.BlockSpec((tp, 128), lambda i: (pos(i), 0)),
                  pl.BlockSpec((1, 128), lambda i: (0, 0)),
                  pl.BlockSpec((1, 128), lambda i: (0, 0))],
        out_specs=[pl.BlockSpec((DA_H, 2, tp, 128), lambda i: (0, 0, i, 0)),
                   pl.BlockSpec((DA_H, tp, 128), lambda i: (0, i, 0)),
                   pl.BlockSpec((DA_H, tp, 128), lambda i: (0, i, 0))],
        out_shape=[jax.ShapeDtypeStruct((DA_H, 2, r, 128), BF16),
                   jax.ShapeDtypeStruct((DA_H, r, 128), BF16),
                   jax.ShapeDtypeStruct((DA_H, r, 128), BF16)],
        compiler_params=_cparams(("parallel",)),
        name="da_prep",
    )(proj, proj, proj, cos_t, sin_t, gq, gk)


def _attn_kernel(q_ref, kc_ref, vc_ref, kl_ref, vl_ref, lam_ref, sg_ref, o_ref, m_ref, l_ref, acc_ref,
                 *, tq, tkc, n_lat_tiles, lam_init):
    q = q_ref[...].reshape(2 * tq, 128)

    def update(kblk, vblk):
        s = _dot_nt(q, kblk)
        m_old = m_ref[...]
        m_new = jnp.maximum(m_old, jnp.max(s, axis=-1, keepdims=True))
        alpha = jnp.exp(m_old - m_new)
        p = jnp.exp(s - m_new)
        l_ref[...] = alpha * l_ref[...] + jnp.sum(p, axis=-1, keepdims=True)
        acc_ref[...] = alpha * acc_ref[...] + _dot(p.astype(BF16), vblk)
        m_ref[...] = m_new

    m_ref[...] = jnp.full_like(m_ref, NEG)
    l_ref[...] = jnp.zeros_like(l_ref)
    acc_ref[...] = jnp.zeros_like(acc_ref)
    update(kc_ref[...], vc_ref[...])

    @pl.when(pl.program_id(2) < n_lat_tiles)
    def _():
        def body(j, carry):
            off = pl.multiple_of(j * tkc, tkc)
            update(kl_ref[pl.ds(off, tkc), :], vl_ref[pl.ds(off, tkc), :])
            return carry

        lax.fori_loop(0, kl_ref.shape[0] // tkc, body, 0)

    lp = lam_ref[...]
    lam = (jnp.exp(jnp.sum(lp[0:1] * lp[1:2], axis=-1, keepdims=True))
           - jnp.exp(jnp.sum(lp[2:3] * lp[3:4], axis=-1, keepdims=True)) + lam_init)
    o = acc_ref[...] / l_ref[...]
    o = o[:tq] - lam * o[tq:]
    o = o * lax.rsqrt(jnp.mean(o * o, axis=-1, keepdims=True) + EPS) * sg_ref[...]
    o_ref[...] = o * (1.0 - lam_init)


def _attn_call(qs, kp, vp, lam_p, subln_g, b, l, lc, with_ctx_queries, lam_init, tq=256, tkc=512):
    r = kp.shape[1]
    assert tq == lc
    ntl = l // tq
    nq = ntl + (1 if with_ctx_queries else 0)
    r_out = r if with_ctx_queries else b * l

    def qrow(bi, qi):
        return jnp.where(qi < ntl, bi * ntl + qi, b * ntl + bi)

    kern = functools.partial(_attn_kernel, tq=tq, tkc=tkc, n_lat_tiles=ntl, lam_init=lam_init)
    return pl.pallas_call(
        kern,
        grid=(b, DA_H, nq),
        in_specs=[pl.BlockSpec((None, 2, tq, 128), lambda bi, h, qi: (h, 0, qrow(bi, qi), 0)),
                  pl.BlockSpec((None, lc, 128), lambda bi, h, qi: (h, b * l // lc + bi, 0)),
                  pl.BlockSpec((None, lc, 128), lambda bi, h, qi: (h, b * l // lc + bi, 0)),
                  pl.BlockSpec((None, l, 128), lambda bi, h, qi: (h, bi, 0)),
                  pl.BlockSpec((None, l, 128), lambda bi, h, qi: (h, bi, 0)),
                  pl.BlockSpec((4, DA_DH), lambda bi, h, qi: (0, 0)),
                  pl.BlockSpec((1, DA_DV), lambda bi, h, qi: (0, 0))],
        out_specs=pl.BlockSpec((tq, 128), lambda bi, h, qi: (qrow(bi, qi), h)),
        out_shape=jax.ShapeDtypeStruct((r_out, HALF), F32),
        scratch_shapes=[pltpu.VMEM((2 * tq, 1), F32), pltpu.VMEM((2 * tq, 1), F32), pltpu.VMEM((2 * tq, 128), F32)],
        compiler_params=_cparams(("parallel", "parallel", "arbitrary")),
        name="attn",
    )(qs, kp, vp, kp, vp, lam_p, subln_g)


def _merge_kernel(of_ref, ob_ref, z_ref, uv_ref, yc_ref, gates_ref, x_ref, mod_ref, gn_ref, vn_ref, ws_ref, bs_ref,
                  wa_ref, wb_ref, wc_ref, wo_ref, o_ref, yb_ref, *, tm):
    oa = of_ref[...] + ob_ref[...]
    z = z_ref[...]
    parts = []
    for h in range(GDN_H):
        oh = oa[:, h * GDN_DV:(h + 1) * GDN_DV]
        parts.append(oh * lax.rsqrt(jnp.mean(oh * oh, axis=-1, keepdims=True) + EPS) * gn_ref[...])
    ya = jnp.concatenate(parts, axis=1) * _silu(z)
    uv = _gelu(uv_ref[...])
    u, v = uv[:, :HALF], uv[:, HALF:]
    v = (v * lax.rsqrt(jnp.mean(v * v, axis=-1, keepdims=True) + EPS) * vn_ref[...]).astype(BF16)
    for ck in range(tm // CMLP_CHUNK):
        rs = slice(ck * CMLP_CHUNK, (ck + 1) * CMLP_CHUNK)
        for g in range(CMLP_G):
            cs = slice(g * CMLP_CH, (g + 1) * CMLP_CH)
            sv = _dot(ws_ref[g], v[rs, cs]) + bs_ref[:, g:g + 1]
            yb_ref[rs, cs] = u[rs, cs] * sv
    yb = yb_ref[...]
    gts = _sigmoid(gates_ref[...])
    y = (gts[:, 0:D] * _dot(ya.astype(BF16), wa_ref[...])
         + gts[:, D:2 * D] * _dot(yb.astype(BF16), wb_ref[...])
         + gts[:, 2 * D:3 * D] * _dot(yc_ref[...].astype(BF16), wc_ref[...]))
    mix = _dot(y.astype(BF16), wo_ref[...])
    o_ref[...] = x_ref[...] + mod_ref[:, 2 * D:3 * D] * mix


def _merge_call(o_f, o_b, proj, yc, xs, mods, gn, vn, ws, bs_t, wa, wb, wc, wo, grp, r_out, tm=256):
    kern = functools.partial(_merge_kernel, tm=tm)
    full = lambda shape: pl.BlockSpec(shape, lambda i: (0,) * len(shape))
    return pl.pallas_call(
        kern,
        grid=(r_out // tm,),
        in_specs=[pl.BlockSpec((tm, HALF), lambda i: (i, 0)),
                  pl.BlockSpec((tm, HALF), lambda i: (i, 0)),
                  pl.BlockSpec((tm, HALF), lambda i: (i, COL_Z // HALF)),
                  pl.BlockSpec((tm, 2 * HALF), lambda i: (i, COL_UV // (2 * HALF))),
                  pl.BlockSpec((tm, HALF), lambda i: (i, 0)),
                  pl.BlockSpec((tm, 3 * D), lambda i: (i, 0)),
                  pl.BlockSpec((tm, D), lambda i: (i, 0)),
                  pl.BlockSpec((None, 1, 6 * D), lambda i: (grp(i * tm), 0, 0)),
                  full((1, GDN_DV)), full((1, HALF)), full((CMLP_G, CMLP_CHUNK, CMLP_CHUNK)),
                  full((CMLP_CHUNK, CMLP_G)),
                  full((HALF, D)), full((HALF, D)), full((HALF, D)), full((D, D))],
        out_specs=pl.BlockSpec((tm, D), lambda i: (i, 0)),
        out_shape=jax.ShapeDtypeStruct((r_out, D), F32),
        scratch_shapes=[pltpu.VMEM((tm, HALF), F32)],
        compiler_params=_cparams(("parallel",)),
        name="merge",
    )(o_f, o_b, proj, proj, yc, proj, xs, mods, gn, vn, ws, bs_t, wa, wb, wc, wo)


def _top_values(s, n):
    rid = lax.broadcasted_iota(jnp.int32, (n, s.shape[1]), 0)
    out = jnp.zeros((n, s.shape[1]), F32)
    for k in range(n):
        m = jnp.max(s, axis=0, keepdims=True)
        out = jnp.where(rid == k, m, out)
        s = jnp.where(s == m, NEG, s)
    return out


def _peer_sel_kernel(x_ref, mod_ref, g_ref, wq_ref, sk_ref, hn_ref, s1_ref, s2_ref, e1_ref, e2_ref, thr_ref):
    x = x_ref[...]
    y = x * lax.rsqrt(jnp.mean(x * x, axis=-1, keepdims=True) + EPS) * g_ref[...]
    hn = (y * (1.0 + mod_ref[:, 4 * D:5 * D]) + mod_ref[:, 3 * D:4 * D]).astype(BF16)
    hn_ref[...] = hn
    q = _dot(hn, wq_ref[...]).astype(BF16)
    for h in range(PEER_H):
        half = PEER_QD // 2
        s1 = _dot_nt(sk_ref[h, 0], q[:, (2 * h) * half:(2 * h + 1) * half])
        s2 = _dot_nt(sk_ref[h, 1], q[:, (2 * h + 1) * half:(2 * h + 2) * half])
        v1 = _top_values(s1, PEER_K)
        v2 = _top_values(s2, PEER_K)
        cand = jnp.concatenate([v1[k:k + 1] + v2 for k in range(PEER_K)], axis=0)
        best = _top_values(cand, PEER_K)
        zsum = jnp.sum(jnp.exp(best - best[0:1]), axis=0, keepdims=True)
        s1_ref[h] = s1
        s2_ref[h] = s2
        e1_ref[h] = jnp.exp(s1 - v1[0:1]) * (1.0 / zsum)
        e2_ref[h] = jnp.exp(s2 - v2[0:1])
        thr_ref[h:h + 1, :] = best[PEER_K - 1:PEER_K]


def _peer_sel_call(xs, mods, g, wq, sk, grp, r_act, ts=256):
    full = lambda shape: pl.BlockSpec(shape, lambda i: (0,) * len(shape))
    sc_spec = pl.BlockSpec((PEER_H, PEER_NK, ts), lambda i: (0, 0, i))
    sc_sd = jax.ShapeDtypeStruct((PEER_H, PEER_NK, r_act), F32)
    return pl.pallas_call(
        _peer_sel_kernel,
        grid=(r_act // ts,),
        in_specs=[pl.BlockSpec((ts, D), lambda i: (i, 0)),
                  pl.BlockSpec((None, 1, 6 * D), lambda i: (grp(i * ts), 0, 0)),
                  full((1, D)), full((D, PEER_H * PEER_QD)), full((PEER_H, 2, PEER_NK, PEER_QD // 2))],
        out_specs=[pl.BlockSpec((ts, D), lambda i: (i, 0)), sc_spec, sc_spec, sc_spec, sc_spec,
                   pl.BlockSpec((PEER_H, ts), lambda i: (0, i))],
        out_shape=[jax.ShapeDtypeStruct((r_act, D), BF16), sc_sd, sc_sd, sc_sd, sc_sd,
                   jax.ShapeDtypeStruct((PEER_H, r_act), F32)],
        compiler_params=_cparams(("parallel",)),
        name="peer_sel",
    )(xs, mods, g, wq, sk)


def _peer_dense_kernel(hn_ref, u_ref, v_ref, s1_ref, s2_ref, e1_ref, e2_ref, thr_ref, x_ref, mod_ref, o_ref, acc_ref,
                       *, te):
    e = pl.program_id(1)

    @pl.when(e == 0)
    def _():
        acc_ref[...] = jnp.zeros_like(acc_ref)

    act = _gelu(_dot_nt(u_ref[...], hn_ref[...]))
    rows = []
    for r in range(te // PEER_NK):
        e1 = e * (te // PEER_NK) + r
        gsum = None
        for h in range(PEER_H):
            a_row = s1_ref[h, pl.ds(e1, 1), :]
            w_row = e1_ref[h, pl.ds(e1, 1), :]
            sel = (a_row + s2_ref[h]) >= thr_ref[h:h + 1, :]
            term = jnp.where(sel, w_row * e2_ref[h], 0.0)
            gsum = term if gsum is None else gsum + term
        rows.append((gsum * act[r * PEER_NK:(r + 1) * PEER_NK]).astype(BF16))
    wt = jnp.concatenate(rows, axis=0)
    acc_ref[...] += _dot_tn(wt, v_ref[...])

    @pl.when(e == pl.num_programs(1) - 1)
    def _():
        o_ref[...] = x_ref[...] + mod_ref[:, 5 * D:6 * D] * acc_ref[...]


def _peer_dense_call(hn, u_tab, v_tab, s1, s2, e1, e2, thr, xs, mods, grp, r_act, tm=512, te=512):
    kern = functools.partial(_peer_dense_kernel, te=te)
    sc_spec = pl.BlockSpec((PEER_H, PEER_NK, tm), lambda i, e: (0, 0, i))
    return pl.pallas_call(
        kern,
        grid=(r_act // tm, PEER_E // te),
        in_specs=[pl.BlockSpec((tm, D), lambda i, e: (i, 0)),
                  pl.BlockSpec((te, D), lambda i, e: (e, 0)),
                  pl.BlockSpec((te, D), lambda i, e: (e, 0)),
                  sc_spec, sc_spec, sc_spec, sc_spec,
                  pl.BlockSpec((PEER_H, tm), lambda i, e: (0, i)),
                  pl.BlockSpec((tm, D), lambda i, e: (i, 0)),
                  pl.BlockSpec((None, 1, 6 * D), lambda i, e: (grp(i * tm), 0, 0))],
        out_specs=pl.BlockSpec((tm, D), lambda i, e: (i, 0)),
        out_shape=jax.ShapeDtypeStruct((r_act, D), F32),
        scratch_shapes=[pltpu.VMEM((tm, D), F32)],
        compiler_params=_cparams(("parallel", "arbitrary")),
        name="peer_dense",
    )(hn, u_tab, v_tab, s1, s2, e1, e2, thr, xs, mods)


def _rope_tables(l, lc):
    rows = l // GRID_W
    r, col = jnp.meshgrid(jnp.arange(rows), jnp.arange(GRID_W), indexing='ij')
    n_freq = DA_DH // 4
    inv = ROPE_BASE ** (-jnp.arange(n_freq, dtype=F32) / n_freq)
    ang = jnp.concatenate([r.reshape(-1, 1).astype(F32) * inv, col.reshape(-1, 1).astype(F32) * inv], axis=-1)
    cos, sin = jnp.cos(ang), jnp.sin(ang)
    cos_t = jnp.concatenate([jnp.tile(cos, (1, 4)), jnp.ones((lc, 128), F32)], axis=0)
    sin_t = jnp.concatenate([jnp.tile(jnp.concatenate([-sin, sin], axis=1), (1, 2)), jnp.zeros((lc, 128), F32)], axis=0)
    return cos_t, sin_t


def kernel(x, c, ctx, c_ctx, w_mod, b_mod, norm1_g, w_in, gdn_conv_w, gdn_a_log, gdn_dt_bias, gdn_out_norm_g,
           cmlp_v_norm_g, cmlp_w_s, cmlp_b_s, da_q_norm_g, da_k_norm_g, da_lambda, da_subln_g, w_branch_a,
           w_branch_b, w_branch_c, w_out, norm2_g, peer_w_q, peer_subkeys, peer_u, peer_v):
    b, l, d = x.shape
    lc = ctx.shape[1]
    depth = w_mod.shape[0]
    bl = b * l
    r_all = bl + b * lc
    assert d == D and l % 512 == 0 and lc == 256 and bl % 512 == 0 and (b * lc) % 512 == 0

    def grp(row):
        return jnp.where(row < bl, row // l, b)

    xs = jnp.concatenate([x.reshape(bl, d), ctx.reshape(b * lc, d)], axis=0)
    cin = jnp.concatenate([c, c_ctx[None, :], jnp.zeros((8 - b - 1, d), F32)], axis=0)
    mods_all = _mod_call(cin, w_mod, b_mod)[:, :b + 1].reshape(depth, b + 1, 1, 6 * d)
    cos_t, sin_t = _rope_tables(l, lc)

    for li in range(depth):
        last = li == depth - 1
        lam_init = 0.8 - 0.6 * math.exp(-0.3 * li)
        mods = mods_all[li]
        w = w_in[li]
        w_re = jnp.concatenate(
            [w[:, 4624:7696], w[:, 0:1536], w[:, 1536:2048], w[:, 2064:3088], w[:, 3088:3600], w[:, 3600:4112],
             w[:, 4112:4624], w[:, 2048:2064], jnp.zeros((d, N_PROJ - COL_BA - 16), F32)], axis=1).astype(BF16)
        proj = _inproj_call(xs, mods, norm1_g[li][None, :], w_re, grp)

        gparams = jnp.zeros((2, 128), F32)
        gparams = gparams.at[0, 8:16].set(gdn_a_log[li].reshape(-1)).at[1, 8:16].set(gdn_dt_bias[li].reshape(-1))
        gq, gk, gv, gbm = _gdn_prep_call(proj, gdn_conv_w[li], gparams, bl, l, lc)
        o_f, o_b = _gdn_scan_call(gq, gk, gv, gbm, b, l, lc)

        gq2 = jnp.tile(da_q_norm_g[li], 2)[None, :]
        gk2 = jnp.tile(da_k_norm_g[li], 2)[None, :]
        qs, kp, vp = _da_prep_call(proj, cos_t, sin_t, gq2, gk2, bl, l)
        yc = _attn_call(qs, kp, vp, da_lambda[li], da_subln_g[li][None, :], b, l, lc, not last, lam_init)

        r_act = bl if last else r_all
        xs = _merge_call(o_f, o_b, proj, yc, xs, mods, gdn_out_norm_g[li][None, :], cmlp_v_norm_g[li][None, :],
                         cmlp_w_s[li].astype(BF16), cmlp_b_s[li].T, w_branch_a[li].astype(BF16),
                         w_branch_b[li].astype(BF16), w_branch_c[li].astype(BF16), w_out[li].astype(BF16), grp, r_act)

        hn, s1, s2, e1, e2, thr = _peer_sel_call(xs, mods, norm2_g[li][None, :], peer_w_q[li].astype(BF16),
                                                 peer_subkeys[li].astype(BF16), grp, r_act)
        xs = _peer_dense_call(hn, peer_u[li].astype(BF16), peer_v[li].astype(BF16), s1, s2, e1, e2, thr, xs, mods,
                              grp, r_act)
    return xs.reshape(b, l, d)
```

```python
import functools
import math

import jax
import jax.numpy as jnp
from jax import lax
from jax.experimental import pallas as pl
from jax.experimental.pallas import tpu as pltpu

F32 = jnp.float32
BF16 = jnp.bfloat16

D = 1024
EPS = 1e-6
GRID_W = 64
GDN_H, GDN_DK, GDN_DV, GDN_CONV, GDN_C = 4, 128, 128, 5, 64
CMLP_G, CMLP_CH, CMLP_CHUNK = 4, 128, 128
DA_H, DA_DH, DA_DV = 4, 64, 128
ROPE_BASE = 10000.0
PEER_H, PEER_NK, PEER_QD, PEER_K = 8, 128, 256, 16
PEER_E = PEER_NK * PEER_NK
HALF = GDN_H * GDN_DV

COL_GATES, COL_QKV, COL_Z, COL_UV, COL_Q, COL_K, COL_V, COL_BA = 0, 3072, 4608, 5120, 6144, 6656, 7168, 7680
N_PROJ = 8192

NEG = -1e30
LOG2E = 1.4426950408889634
VMEM_LIMIT = 56 * 1024 * 1024


def _cparams(sem):
    return pltpu.CompilerParams(dimension_semantics=sem, vmem_limit_bytes=VMEM_LIMIT)


def _silu(x):
    return x * (1.0 / (1.0 + jnp.exp(-x)))


def _sigmoid(x):
    return 1.0 / (1.0 + jnp.exp(-x))


def _gelu(x):
    return 0.5 * x * (1.0 + jnp.tanh(0.7978845608028654 * (x + 0.044715 * (x * x * x))))


def _dot(a, b):
    return jnp.dot(a, b, preferred_element_type=F32)


def _dot_nt(a, b):
    return lax.dot_general(a, b, (((1,), (1,)), ((), ())), preferred_element_type=F32)


def _dot_tn(a, b):
    return lax.dot_general(a, b, (((0,), (0,)), ((), ())), preferred_element_type=F32)


def _mod_kernel(c_ref, w_ref, b_ref, o_ref):
    o_ref[...] = jnp.dot(_silu(c_ref[...]), w_ref[...], preferred_element_type=F32,
                         precision=lax.Precision.HIGHEST) + b_ref[...]


def _mod_call(cin, w_mod, b_mod):
    depth = w_mod.shape[0]
    tn = 1024
    return pl.pallas_call(
        _mod_kernel,
        grid=(depth, 6 * D // tn),
        in_specs=[pl.BlockSpec((8, D), lambda l, j: (0, 0)),
                  pl.BlockSpec((None, D, tn), lambda l, j: (l, 0, j)),
                  pl.BlockSpec((None, 1, tn), lambda l, j: (l, 0, j))],
        out_specs=pl.BlockSpec((None, 8, tn), lambda l, j: (l, 0, j)),
        out_shape=jax.ShapeDtypeStruct((depth, 8, 6 * D), F32),
        compiler_params=_cparams(("parallel", "parallel")),
        name="mod",
    )(cin, w_mod, b_mod.reshape(depth, 1, 6 * D))


def _inproj_kernel(x_ref, mod_ref, g_ref, w_ref, o_ref, hn_ref):
    @pl.when(pl.program_id(1) == 0)
    def _():
        x = x_ref[...]
        y = x * lax.rsqrt(jnp.mean(x * x, axis=-1, keepdims=True) + EPS) * g_ref[...]
        hn_ref[...] = (y * (1.0 + mod_ref[:, D:2 * D]) + mod_ref[:, 0:D]).astype(BF16)

    o_ref[...] = _dot(hn_ref[...], w_ref[...])


def _inproj_call(xs, mods, g, w, grp, tm=512, tn=512):
    r = xs.shape[0]
    return pl.pallas_call(
        _inproj_kernel,
        grid=(r // tm, N_PROJ // tn),
        in_specs=[pl.BlockSpec((tm, D), lambda i, j: (i, 0)),
                  pl.BlockSpec((None, 1, 6 * D), lambda i, j: (grp(i * tm), 0, 0)),
                  pl.BlockSpec((1, D), lambda i, j: (0, 0)),
                  pl.BlockSpec((D, tn), lambda i, j: (0, j))],
        out_specs=pl.BlockSpec((tm, tn), lambda i, j: (i, j)),
        out_shape=jax.ShapeDtypeStruct((r, N_PROJ), F32),
        scratch_shapes=[pltpu.VMEM((tm, D), BF16)],
        compiler_params=_cparams(("parallel", "arbitrary")),
        name="inproj",
    )(xs, mods, g, w)


def _gdn_prep_kernel(x_ref, prev_ref, next_ref, ba_ref, cw_ref, gp_ref,
                     q_ref, k_ref, v_ref, gb_ref, ext_ref, *, tp, bl, l, lc):
    i = pl.program_id(0)
    r0 = i * tp
    r1 = r0 + tp
    is_start = jnp.where(r0 < bl, r0 % l == 0, (r0 - bl) % lc == 0)
    is_end = jnp.where(r1 <= bl, r1 % l == 0, (r1 - bl) % lc == 0)
    ext_ref[0:8, :] = jnp.where(is_start, 0.0, prev_ref[...])
    ext_ref[8:8 + tp, :] = x_ref[...]
    ext_ref[8 + tp:16 + tp, :] = jnp.where(is_end, 0.0, next_ref[...])
    y = cw_ref[0:1, :] * ext_ref[6:6 + tp, :]
    for j in range(1, GDN_CONV):
        y = y + cw_ref[j:j + 1, :] * ext_ref[6 + j:6 + j + tp, :]
    y = _silu(y)
    for h in range(GDN_H):
        qh = y[:, h * GDN_DK:(h + 1) * GDN_DK]
        q_ref[:, h * GDN_DK:(h + 1) * GDN_DK] = qh * (
            lax.rsqrt(jnp.sum(qh * qh, axis=-1, keepdims=True) + EPS) * (GDN_DK ** -0.5))
        kh = y[:, HALF + h * GDN_DK:HALF + (h + 1) * GDN_DK]
        k_ref[:, h * GDN_DK:(h + 1) * GDN_DK] = kh * lax.rsqrt(jnp.sum(kh * kh, axis=-1, keepdims=True) + EPS)
    v_ref[...] = y[:, 2 * HALF:3 * HALF]
    ba = ba_ref[...]
    beta = _sigmoid(ba)
    t = ba + gp_ref[1:2, :]
    sp = jnp.maximum(t, 0.0) + jnp.log(1.0 + jnp.exp(-jnp.abs(t)))
    g = -jnp.exp(gp_ref[0:1, :]) * sp
    lane = lax.broadcasted_iota(jnp.int32, ba.shape, 1)
    gb_ref[...] = jnp.where(lane < 2 * GDN_H, beta, g)


def _gdn_prep_call(proj, conv_w, gparams, bl, l, lc, tp=256):
    r = proj.shape[0]
    nb8 = r // 8
    qkv_blk = COL_QKV // (3 * HALF)
    kern = functools.partial(_gdn_prep_kernel, tp=tp, bl=bl, l=l, lc=lc)
    out_sd = jax.ShapeDtypeStruct((r, HALF), F32)
    return pl.pallas_call(
        kern,
        grid=(r // tp,),
        in_specs=[pl.BlockSpec((tp, 3 * HALF), lambda i: (i, qkv_blk)),
                  pl.BlockSpec((8, 3 * HALF), lambda i: (jnp.maximum(i * (tp // 8) - 1, 0), qkv_blk)),
                  pl.BlockSpec((8, 3 * HALF), lambda i: (jnp.minimum((i + 1) * (tp // 8), nb8 - 1), qkv_blk)),
                  pl.BlockSpec((tp, 128), lambda i: (i, COL_BA // 128)),
                  pl.BlockSpec((GDN_CONV, 3 * HALF), lambda i: (0, 0)),
                  pl.BlockSpec((2, 128), lambda i: (0, 0))],
        out_specs=[pl.BlockSpec((tp, HALF), lambda i: (i, 0)),
                   pl.BlockSpec((tp, HALF), lambda i: (i, 0)),
                   pl.BlockSpec((tp, HALF), lambda i: (i, 0)),
                   pl.BlockSpec((tp, 128), lambda i: (i, 0))],
        out_shape=[out_sd, out_sd, out_sd, jax.ShapeDtypeStruct((r, 128), F32)],
        scratch_shapes=[pltpu.VMEM((tp + 16, 3 * HALF), F32)],
        compiler_params=_cparams(("parallel",)),
        name="gdn_prep",
    )(proj, proj, proj, proj, conv_w, gparams)


def _gdn_chain(q, k, v, gb, d, h, s_ref):
    c = GDN_C
    row = lax.broadcasted_iota(jnp.int32, (c, c), 0)
    col = lax.broadcasted_iota(jnp.int32, (c, c), 1)
    if d == 0:
        incl, strict = row >= col, row > col
    else:
        incl, strict = row <= col, row < col
    hp = lax.Precision.HIGHEST
    gc_col = jnp.dot(jnp.where(incl, 1.0, 0.0), gb, preferred_element_type=F32, precision=hp)
    tri_t = jnp.where(row <= col if d == 0 else row >= col, 1.0, 0.0)
    gc_row = lax.dot_general(gb, tri_t, (((0,), (0,)), ((), ())), preferred_element_type=F32,
                             precision=hp)
    cb, cg = d * GDN_H + h, 2 * GDN_H + d * GDN_H + h
    gcol = gc_col[:, cg:cg + 1]
    grow = gc_row[cg:cg + 1, :]
    beta = gb[:, cb:cb + 1]
    last = c - 1 if d == 0 else 0
    glast = gc_col[last:last + 1, cg:cg + 1]

    decay = jnp.where(incl, jnp.exp(jnp.where(incl, gcol - grow, 0.0)), 0.0)
    kb = k.astype(BF16)
    kk = _dot_nt(kb, kb)
    x = jnp.where(strict, -(kk * decay * beta), 0.0)
    eye = jnp.where(row == col, 1.0, 0.0)
    tinv = eye + x
    p = x
    for _ in range(5):
        pb = p.astype(BF16)
        p = _dot(pb, pb)
        tinv = tinv + _dot(tinv.astype(BF16), p.astype(BF16))
    eg = jnp.exp(gcol)
    rhs = jnp.concatenate([v * beta, k * (beta * eg)], axis=1).astype(BF16)
    uw = _dot(tinv.astype(BF16), rhs)
    u, w = uw[:, :GDN_DV], uw[:, GDN_DV:]
    qk = _dot_nt(q.astype(BF16), kb) * decay
    q_dec = q * eg
    k_dec = k * jnp.exp(glast - gcol)
    s = s_ref[d, h]
    sb = s.astype(BF16)
    ws = _dot(jnp.concatenate([w, q_dec], axis=0).astype(BF16), sb)
    v_new = u - ws[:c]
    vb = v_new.astype(BF16)
    o = ws[c:] + _dot(qk.astype(BF16), vb)
    s_ref[d, h] = s * jnp.exp(glast) + _dot_tn(k_dec.astype(BF16), vb)
    return o


def _gdn_scan_kernel(qf, kf, vf, gf, qb, kb, vb, gb, of_ref, ob_ref, s_ref):
    @pl.when(pl.program_id(1) == 0)
    def _():
        s_ref[...] = jnp.zeros_like(s_ref)

    for d, (q_ref, k_ref, v_ref, g_ref, o_ref) in enumerate(((qf, kf, vf, gf, of_ref), (qb, kb, vb, gb, ob_ref))):
        g = g_ref[...]
        for h in range(GDN_H):
            sl = slice(h * GDN_DK, (h + 1) * GDN_DK)
            o_ref[:, sl] = _gdn_chain(q_ref[:, sl], k_ref[:, sl], v_ref[:, sl], g, d, h, s_ref)


def _gdn_scan_call(q, k, v, gbm, b, l, lc):
    r = q.shape[0]
    c = GDN_C
    ncc, ncl = lc // c, l // c
    base = b * l // c

    def fwd(bi, s):
        return jnp.where(s < ncc, base + bi * ncc + s, bi * ncl + s - ncc)

    def bwd(bi, s):
        return jnp.where(s < ncc, base + bi * ncc + (ncc - 1 - s), bi * ncl + (ncl - 1) - (s - ncc))

    def specs(fn):
        return [pl.BlockSpec((c, HALF), lambda bi, s: (fn(bi, s), 0))] * 3 + [
            pl.BlockSpec((c, 128), lambda bi, s: (fn(bi, s), 0))]

    out_sd = jax.ShapeDtypeStruct((r, HALF), F32)
    return pl.pallas_call(
        _gdn_scan_kernel,
        grid=(b, ncc + ncl),
        in_specs=specs(fwd) + specs(bwd),
        out_specs=[pl.BlockSpec((c, HALF), lambda bi, s: (fwd(bi, s), 0)),
                   pl.BlockSpec((c, HALF), lambda bi, s: (bwd(bi, s), 0))],
        out_shape=[out_sd, out_sd],
        scratch_shapes=[pltpu.VMEM((2, GDN_H, GDN_DK, GDN_DV), F32)],
        compiler_params=_cparams(("parallel", "arbitrary")),
        name="gdn_scan",
    )(q, k, v, gbm, q, k, v, gbm)


def _da_prep_kernel(q_ref, k_ref, v_ref, cos_ref, sin_ref, gq_ref, gk_ref, qs_ref, kp_ref, vp_ref):
    cs, sn = cos_ref[...], sin_ref[...]
    lane = lax.broadcasted_iota(jnp.int32, cs.shape, 1)
    lo64 = lane < DA_DH
    lo32 = (lane % DA_DH) < (DA_DH // 2)

    def norm_rope(t, g):
        sq = t * t
        ms0 = jnp.sum(jnp.where(lo64, sq, 0.0), axis=-1, keepdims=True)
        ms1 = jnp.sum(jnp.where(lo64, 0.0, sq), axis=-1, keepdims=True)
        inv = lax.rsqrt(jnp.where(lo64, ms0, ms1) * (1.0 / DA_DH) + EPS)
        t = t * inv * g
        swapped = jnp.where(lo32, pltpu.roll(t, 128 - DA_DH // 2, 1), pltpu.roll(t, DA_DH // 2, 1))
        return t * cs + swapped * sn

    for h in range(DA_H):
        sl = slice(h * 128, (h + 1) * 128)
        qh = norm_rope(q_ref[:, sl], gq_ref[...]) * (DA_DH ** -0.5 * LOG2E)
        qs_ref[h, 0] = jnp.where(lo64, qh, 0.0).astype(BF16)
        qs_ref[h, 1] = jnp.where(lo64, 0.0, qh).astype(BF16)
        kp_ref[h] = norm_rope(k_ref[:, sl], gk_ref[...]).astype(BF16)
        vp_ref[h] = v_ref[:, sl].T.astype(BF16)


def _da_prep_call(proj, cos_t, sin_t, gq, gk, bl, l, tp=256):
    r = proj.shape[0]
    ntl = l // tp

    def pos(i):
        return jnp.where(i * tp < bl, i % ntl, ntl)

    return pl.pallas_call(
        _da_prep_kernel,
        grid=(r // tp,),
        in_specs=[pl.BlockSpec((tp, HALF), lambda i: (i, COL_Q // HALF)),
                  pl.BlockSpec((tp, HALF), lambda i: (i, COL_K // HALF)),
                  pl.BlockSpec((tp, HALF), lambda i: (i, COL_V // HALF)),
                  pl.BlockSpec((tp, 128), lambda i: (pos(i), 0)),
                  pl.BlockSpec((tp, 128), lambda i: (pos(i), 0)),
                  pl.BlockSpec((1, 128), lambda i: (0, 0)),
                  pl.BlockSpec((1, 128), lambda i: (0, 0))],
        out_specs=[pl.BlockSpec((DA_H, 2, tp, 128), lambda i: (0, 0, i, 0)),
                   pl.BlockSpec((DA_H, tp, 128), lambda i: (0, i, 0)),
                   pl.BlockSpec((DA_H, DA_DV, tp), lambda i: (0, 0, i))],
        out_shape=[jax.ShapeDtypeStruct((DA_H, 2, r, 128), BF16),
                   jax.ShapeDtypeStruct((DA_H, r, 128), BF16),
                   jax.ShapeDtypeStruct((DA_H, DA_DV, r), BF16)],
        compiler_params=_cparams(("parallel",)),
        name="da_prep",
    )(proj, proj, proj, cos_t, sin_t, gq, gk)


def _attn_kernel(q_ref, kc_ref, vc_ref, kl_ref, vl_ref, lam_ref, sg_ref, o_ref, m_ref, l_ref, acc_ref,
                 s0_ref, s1_ref, p0_ref, p1_ref, cm_ref, al_ref, *, tq, tkc, n_lat_tiles, lam_init):
    q = q_ref[...].reshape(2 * tq, 128)

    st = _dot_nt(kc_ref[...], q)
    m0 = jnp.max(st, axis=0, keepdims=True)
    p = jnp.exp2(st - m0)
    m_ref[...] = m0
    l_ref[...] = jnp.sum(p, axis=0, keepdims=True)
    acc_ref[...] = _dot(vc_ref[...], p.astype(BF16))

    def scores(j, s_ref, slot):
        off = pl.multiple_of(j * tkc, tkc)
        st = _dot_nt(kl_ref[pl.ds(off, tkc), :], q)
        s_ref[...] = st
        cm_ref[slot:slot + 1, :] = jnp.max(st, axis=0, keepdims=True)

    def softmax(s_ref, p_ref, slot):
        m_old = m_ref[...]
        m_new = jnp.maximum(m_old, cm_ref[slot:slot + 1, :])
        alpha = jnp.exp2(m_old - m_new)
        p = jnp.exp2(s_ref[...] - m_new)
        l_ref[...] = alpha * l_ref[...] + jnp.sum(p, axis=0, keepdims=True)
        p_ref[...] = p.astype(BF16)
        al_ref[slot:slot + 1, :] = alpha
        m_ref[...] = m_new

    def weighted_values(j, p_ref, slot):
        off = pl.multiple_of(j * tkc, tkc)
        acc_ref[...] = al_ref[slot:slot + 1, :] * acc_ref[...] + _dot(vl_ref[:, pl.ds(off, tkc)], p_ref[...])

    @pl.when(pl.program_id(2) < n_lat_tiles)
    def _():
        n = kl_ref.shape[0] // tkc
        scores(0, s0_ref, 0)
        p1_ref[...] = jnp.zeros_like(p1_ref)
        al_ref[1:2, :] = jnp.ones((1, 2 * tq), F32)

        def body(jj, carry):
            j = 2 * jj
            scores(j + 1, s1_ref, 1)
            softmax(s0_ref, p0_ref, 0)
            weighted_values(jnp.maximum(j - 1, 0), p1_ref, 1)
            scores(jnp.minimum(j + 2, n - 1), s0_ref, 0)
            softmax(s1_ref, p1_ref, 1)
            weighted_values(j, p0_ref, 0)
            return carry

        lax.fori_loop(0, n // 2, body, 0)
        weighted_values(n - 1, p1_ref, 1)

    lp = lam_ref[...]
    lam = (jnp.exp(jnp.sum(lp[0:1] * lp[1:2], axis=-1, keepdims=True))
           - jnp.exp(jnp.sum(lp[2:3] * lp[3:4], axis=-1, keepdims=True)) + lam_init)
    ot = acc_ref[...] * (1.0 / l_ref[...])
    ot = ot[:, :tq] - lam * ot[:, tq:]
    ot = ot * lax.rsqrt(jnp.mean(ot * ot, axis=0, keepdims=True) + EPS) * sg_ref[...]
    o_ref[...] = (ot * (1.0 - lam_init)).T


def _attn_call(qs, kp, vpt, lam_p, subln_g, b, l, lc, with_ctx_queries, lam_init, tq=256, tkc=1024):
    r = kp.shape[1]
    assert tq == lc and l % (2 * tkc) == 0
    ntl = l // tq
    nq = ntl + (1 if with_ctx_queries else 0)
    r_out = r if with_ctx_queries else b * l

    def qrow(bi, qi):
        return jnp.where(qi < ntl, bi * ntl + qi, b * ntl + bi)

    kern = functools.partial(_attn_kernel, tq=tq, tkc=tkc, n_lat_tiles=ntl, lam_init=lam_init)
    return pl.pallas_call(
        kern,
        grid=(b, DA_H, nq),
        in_specs=[pl.BlockSpec((None, 2, tq, 128), lambda bi, h, qi: (h, 0, qrow(bi, qi), 0)),
                  pl.BlockSpec((None, lc, 128), lambda bi, h, qi: (h, b * l // lc + bi, 0)),
                  pl.BlockSpec((None, DA_DV, lc), lambda bi, h, qi: (h, 0, b * l // lc + bi)),
                  pl.BlockSpec((None, l, 128), lambda bi, h, qi: (h, bi, 0)),
                  pl.BlockSpec((None, DA_DV, l), lambda bi, h, qi: (h, 0, bi)),
                  pl.BlockSpec((4, DA_DH), lambda bi, h, qi: (0, 0)),
                  pl.BlockSpec((DA_DV, 1), lambda bi, h, qi: (0, 0))],
        out_specs=pl.BlockSpec((tq, 128), lambda bi, h, qi: (qrow(bi, qi), h)),
        out_shape=jax.ShapeDtypeStruct((r_out, HALF), F32),
        scratch_shapes=[pltpu.VMEM((1, 2 * tq), F32), pltpu.VMEM((1, 2 * tq), F32), pltpu.VMEM((DA_DV, 2 * tq), F32),
                        pltpu.VMEM((tkc, 2 * tq), F32), pltpu.VMEM((tkc, 2 * tq), F32),
                        pltpu.VMEM((tkc, 2 * tq), BF16), pltpu.VMEM((tkc, 2 * tq), BF16),
                        pltpu.VMEM((8, 2 * tq), F32), pltpu.VMEM((8, 2 * tq), F32)],
        compiler_params=_cparams(("parallel", "parallel", "arbitrary")),
        name="attn",
    )(qs, kp, vpt, kp, vpt, lam_p, subln_g)


def _merge_kernel(of_ref, ob_ref, z_ref, uv_ref, yc_ref, gates_ref, x_ref, mod_ref, gn_ref, vn_ref, ws_ref, bs_ref,
                  wa_ref, wb_ref, wc_ref, wo_ref, o_ref, yb_ref, *, tm):
    oa = of_ref[...] + ob_ref[...]
    z = z_ref[...]
    parts = []
    for h in range(GDN_H):
        oh = oa[:, h * GDN_DV:(h + 1) * GDN_DV]
        parts.append(oh * lax.rsqrt(jnp.mean(oh * oh, axis=-1, keepdims=True) + EPS) * gn_ref[...])
    ya = jnp.concatenate(parts, axis=1) * _silu(z)
    uv = _gelu(uv_ref[...])
    u, v = uv[:, :HALF], uv[:, HALF:]
    v = (v * lax.rsqrt(jnp.mean(v * v, axis=-1, keepdims=True) + EPS) * vn_ref[...]).astype(BF16)
    for ck in range(tm // CMLP_CHUNK):
        rs = slice(ck * CMLP_CHUNK, (ck + 1) * CMLP_CHUNK)
        for g in range(CMLP_G):
            cs = slice(g * CMLP_CH, (g + 1) * CMLP_CH)
            sv = _dot(ws_ref[g], v[rs, cs]) + bs_ref[:, g:g + 1]
            yb_ref[rs, cs] = u[rs, cs] * sv
    yb = yb_ref[...]
    gts = _sigmoid(gates_ref[...])
    y = (gts[:, 0:D] * _dot(ya.astype(BF16), wa_ref[...])
         + gts[:, D:2 * D] * _dot(yb.astype(BF16), wb_ref[...])
         + gts[:, 2 * D:3 * D] * _dot(yc_ref[...].astype(BF16), wc_ref[...]))
    mix = _dot(y.astype(BF16), wo_ref[...])
    o_ref[...] = x_ref[...] + mod_ref[:, 2 * D:3 * D] * mix


def _merge_call(o_f, o_b, proj, yc, xs, mods, gn, vn, ws, bs_t, wa, wb, wc, wo, grp, r_out, tm=256):
    kern = functools.partial(_merge_kernel, tm=tm)
    full = lambda shape: pl.BlockSpec(shape, lambda i: (0,) * len(shape))
    return pl.pallas_call(
        kern,
        grid=(r_out // tm,),
        in_specs=[pl.BlockSpec((tm, HALF), lambda i: (i, 0)),
                  pl.BlockSpec((tm, HALF), lambda i: (i, 0)),
                  pl.BlockSpec((tm, HALF), lambda i: (i, COL_Z // HALF)),
                  pl.BlockSpec((tm, 2 * HALF), lambda i: (i, COL_UV // (2 * HALF))),
                  pl.BlockSpec((tm, HALF), lambda i: (i, 0)),
                  pl.BlockSpec((tm, 3 * D), lambda i: (i, 0)),
                  pl.BlockSpec((tm, D), lambda i: (i, 0)),
                  pl.BlockSpec((None, 1, 6 * D), lambda i: (grp(i * tm), 0, 0)),
                  full((1, GDN_DV)), full((1, HALF)), full((CMLP_G, CMLP_CHUNK, CMLP_CHUNK)),
                  full((CMLP_CHUNK, CMLP_G)),
                  full((HALF, D)), full((HALF, D)), full((HALF, D)), full((D, D))],
        out_specs=pl.BlockSpec((tm, D), lambda i: (i, 0)),
        out_shape=jax.ShapeDtypeStruct((r_out, D), F32),
        scratch_shapes=[pltpu.VMEM((tm, HALF), F32)],
        compiler_params=_cparams(("parallel",)),
        name="merge",
    )(o_f, o_b, proj, proj, yc, proj, xs, mods, gn, vn, ws, bs_t, wa, wb, wc, wo)


def _top_values(s, n):
    rid = lax.broadcasted_iota(jnp.int32, (n, s.shape[1]), 0)
    out = jnp.zeros((n, s.shape[1]), F32)
    for k in range(n):
        m = jnp.max(s, axis=0, keepdims=True)
        out = jnp.where(rid == k, m, out)
        s = jnp.where(s == m, NEG, s)
    return out


def _peer_sel_kernel(x_ref, mod_ref, g_ref, wq_ref, sk_ref, hn_ref, s1_ref, s2_ref, e1_ref, e2_ref, thr_ref):
    x = x_ref[...]
    y = x * lax.rsqrt(jnp.mean(x * x, axis=-1, keepdims=True) + EPS) * g_ref[...]
    hn = (y * (1.0 + mod_ref[:, 4 * D:5 * D]) + mod_ref[:, 3 * D:4 * D]).astype(BF16)
    hn_ref[...] = hn
    q = _dot(hn, wq_ref[...]).astype(BF16)
    for h in range(PEER_H):
        half = PEER_QD // 2
        s1 = _dot_nt(sk_ref[h, 0], q[:, (2 * h) * half:(2 * h + 1) * half])
        s2 = _dot_nt(sk_ref[h, 1], q[:, (2 * h + 1) * half:(2 * h + 2) * half])
        v1 = _top_values(s1, PEER_K)
        v2 = _top_values(s2, PEER_K)
        cand = jnp.concatenate([v1[k:k + 1] + v2 for k in range(PEER_K)], axis=0)
        best = _top_values(cand, PEER_K)
        zsum = jnp.sum(jnp.exp(best - best[0:1]), axis=0, keepdims=True)
        s1_ref[h] = s1
        s2_ref[h] = s2
        e1_ref[h] = jnp.exp(s1 - v1[0:1]) * (1.0 / zsum)
        e2_ref[h] = jnp.exp(s2 - v2[0:1])
        thr_ref[h:h + 1, :] = best[PEER_K - 1:PEER_K]


def _peer_sel_call(xs, mods, g, wq, sk, grp, r_act, ts=256):
    full = lambda shape: pl.BlockSpec(shape, lambda i: (0,) * len(shape))
    sc_spec = pl.BlockSpec((PEER_H, PEER_NK, ts), lambda i: (0, 0, i))
    sc_sd = jax.ShapeDtypeStruct((PEER_H, PEER_NK, r_act), F32)
    return pl.pallas_call(
        _peer_sel_kernel,
        grid=(r_act // ts,),
        in_specs=[pl.BlockSpec((ts, D), lambda i: (i, 0)),
                  pl.BlockSpec((None, 1, 6 * D), lambda i: (grp(i * ts), 0, 0)),
                  full((1, D)), full((D, PEER_H * PEER_QD)), full((PEER_H, 2, PEER_NK, PEER_QD // 2))],
        out_specs=[pl.BlockSpec((ts, D), lambda i: (i, 0)), sc_spec, sc_spec, sc_spec, sc_spec,
                   pl.BlockSpec((PEER_H, ts), lambda i: (0, i))],
        out_shape=[jax.ShapeDtypeStruct((r_act, D), BF16), sc_sd, sc_sd, sc_sd, sc_sd,
                   jax.ShapeDtypeStruct((PEER_H, r_act), F32)],
        compiler_params=_cparams(("parallel",)),
        name="peer_sel",
    )(xs, mods, g, wq, sk)


def _peer_dense_kernel(hn_ref, u_ref, v_ref, s1_ref, s2_ref, e1_ref, e2_ref, thr_ref, x_ref, mod_ref, o_ref, acc_ref,
                       *, te):
    e = pl.program_id(1)

    @pl.when(e == 0)
    def _():
        acc_ref[...] = jnp.zeros_like(acc_ref)

    act = _gelu(_dot_nt(u_ref[...], hn_ref[...]))
    rows = []
    for r in range(te // PEER_NK):
        e1 = e * (te // PEER_NK) + r
        gsum = None
        for h in range(PEER_H):
            a_row = s1_ref[h, pl.ds(e1, 1), :]
            w_row = e1_ref[h, pl.ds(e1, 1), :]
            sel = (a_row + s2_ref[h]) >= thr_ref[h:h + 1, :]
            term = jnp.where(sel, w_row * e2_ref[h], 0.0)
            gsum = term if gsum is None else gsum + term
        rows.append((gsum * act[r * PEER_NK:(r + 1) * PEER_NK]).astype(BF16))
    wt = jnp.concatenate(rows, axis=0)
    acc_ref[...] += _dot_tn(wt, v_ref[...])

    @pl.when(e == pl.num_programs(1) - 1)
    def _():
        o_ref[...] = x_ref[...] + mod_ref[:, 5 * D:6 * D] * acc_ref[...]


def _peer_dense_call(hn, u_tab, v_tab, s1, s2, e1, e2, thr, xs, mods, grp, r_act, tm=512, te=512):
    kern = functools.partial(_peer_dense_kernel, te=te)
    sc_spec = pl.BlockSpec((PEER_H, PEER_NK, tm), lambda i, e: (0, 0, i))
    return pl.pallas_call(
        kern,
        grid=(r_act // tm, PEER_E // te),
        in_specs=[pl.BlockSpec((tm, D), lambda i, e: (i, 0)),
                  pl.BlockSpec((te, D), lambda i, e: (e, 0)),
                  pl.BlockSpec((te, D), lambda i, e: (e, 0)),
                  sc_spec, sc_spec, sc_spec, sc_spec,
                  pl.BlockSpec((PEER_H, tm), lambda i, e: (0, i)),
                  pl.BlockSpec((tm, D), lambda i, e: (i, 0)),
                  pl.BlockSpec((None, 1, 6 * D), lambda i, e: (grp(i * tm), 0, 0))],
        out_specs=pl.BlockSpec((tm, D), lambda i, e: (i, 0)),
        out_shape=jax.ShapeDtypeStruct((r_act, D), F32),
        scratch_shapes=[pltpu.VMEM((tm, D), F32)],
        compiler_params=_cparams(("parallel", "arbitrary")),
        name="peer_dense",
    )(hn, u_tab, v_tab, s1, s2, e1, e2, thr, xs, mods)


def _rope_tables(l, lc):
    rows = l // GRID_W
    r, col = jnp.meshgrid(jnp.arange(rows), jnp.arange(GRID_W), indexing='ij')
    n_freq = DA_DH // 4
    inv = ROPE_BASE ** (-jnp.arange(n_freq, dtype=F32) / n_freq)
    ang = jnp.concatenate([r.reshape(-1, 1).astype(F32) * inv, col.reshape(-1, 1).astype(F32) * inv], axis=-1)
    cos, sin = jnp.cos(ang), jnp.sin(ang)
    cos_t = jnp.concatenate([jnp.tile(cos, (1, 4)), jnp.ones((lc, 128), F32)], axis=0)
    sin_t = jnp.concatenate([jnp.tile(jnp.concatenate([-sin, sin], axis=1), (1, 2)), jnp.zeros((lc, 128), F32)], axis=0)
    return cos_t, sin_t


def kernel(x, c, ctx, c_ctx, w_mod, b_mod, norm1_g, w_in, gdn_conv_w, gdn_a_log, gdn_dt_bias, gdn_out_norm_g,
           cmlp_v_norm_g, cmlp_w_s, cmlp_b_s, da_q_norm_g, da_k_norm_g, da_lambda, da_subln_g, w_branch_a,
           w_branch_b, w_branch_c, w_out, norm2_g, peer_w_q, peer_subkeys, peer_u, peer_v):
    b, l, d = x.shape
    lc = ctx.shape[1]
    depth = w_mod.shape[0]
    bl = b * l
    r_all = bl + b * lc
    assert d == D and l % 512 == 0 and lc == 256 and bl % 512 == 0 and (b * lc) % 512 == 0

    def grp(row):
        return jnp.where(row < bl, row // l, b)

    xs = jnp.concatenate([x.reshape(bl, d), ctx.reshape(b * lc, d)], axis=0)
    cin = jnp.concatenate([c, c_ctx[None, :], jnp.zeros((8 - b - 1, d), F32)], axis=0)
    mods_all = _mod_call(cin, w_mod, b_mod)[:, :b + 1].reshape(depth, b + 1, 1, 6 * d)
    cos_t, sin_t = _rope_tables(l, lc)

    for li in range(depth):
        last = li == depth - 1
        lam_init = 0.8 - 0.6 * math.exp(-0.3 * li)
        mods = mods_all[li]
        w = w_in[li]
        w_re = jnp.concatenate(
            [w[:, 4624:7696], w[:, 0:1536], w[:, 1536:2048], w[:, 2064:3088], w[:, 3088:3600], w[:, 3600:4112],
             w[:, 4112:4624], w[:, 2048:2064], jnp.zeros((d, N_PROJ - COL_BA - 16), F32)], axis=1).astype(BF16)
        proj = _inproj_call(xs, mods, norm1_g[li][None, :], w_re, grp)

        gparams = jnp.zeros((2, 128), F32)
        gparams = gparams.at[0, 8:16].set(gdn_a_log[li].reshape(-1)).at[1, 8:16].set(gdn_dt_bias[li].reshape(-1))
        gq, gk, gv, gbm = _gdn_prep_call(proj, gdn_conv_w[li], gparams, bl, l, lc)
        o_f, o_b = _gdn_scan_call(gq, gk, gv, gbm, b, l, lc)

        gq2 = jnp.tile(da_q_norm_g[li], 2)[None, :]
        gk2 = jnp.tile(da_k_norm_g[li], 2)[None, :]
        qs, kp, vp = _da_prep_call(proj, cos_t, sin_t, gq2, gk2, bl, l)
        yc = _attn_call(qs, kp, vp, da_lambda[li], da_subln_g[li][:, None], b, l, lc, not last, lam_init)

        r_act = bl if last else r_all
        xs = _merge_call(o_f, o_b, proj, yc, xs, mods, gdn_out_norm_g[li][None, :], cmlp_v_norm_g[li][None, :],
                         cmlp_w_s[li].astype(BF16), cmlp_b_s[li].T, w_branch_a[li].astype(BF16),
                         w_branch_b[li].astype(BF16), w_branch_c[li].astype(BF16), w_out[li].astype(BF16), grp, r_act)

        hn, s1, s2, e1, e2, thr = _peer_sel_call(xs, mods, norm2_g[li][None, :], peer_w_q[li].astype(BF16),
                                                 peer_subkeys[li].astype(BF16), grp, r_act)
        xs = _peer_dense_call(hn, peer_u[li].astype(BF16), peer_v[li].astype(BF16), s1, s2, e1, e2, thr, xs, mods,
                              grp, r_act)
    return xs.reshape(b, l, d)
```

```python
import functools
import math

import jax
import jax.numpy as jnp
from jax import lax
from jax.experimental import pallas as pl
from jax.experimental.pallas import tpu as pltpu

F32 = jnp.float32
BF16 = jnp.bfloat16

D = 1024
EPS = 1e-6
GRID_W = 64
GDN_H, GDN_DK, GDN_DV, GDN_CONV, GDN_C = 4, 128, 128, 5, 64
CMLP_G, CMLP_CH, CMLP_CHUNK = 4, 128, 128
DA_H, DA_DH, DA_DV = 4, 64, 128
ROPE_BASE = 10000.0
PEER_H, PEER_NK, PEER_QD, PEER_K = 8, 128, 256, 16
PEER_E = PEER_NK * PEER_NK
HALF = GDN_H * GDN_DV

COL_GATES, COL_QKV, COL_Z, COL_UV, COL_Q, COL_K, COL_V, COL_BA = 0, 3072, 4608, 5120, 6144, 6656, 7168, 7680
N_PROJ = 8192

NEG = -1e30
LOG2E = 1.4426950408889634
VMEM_LIMIT = 56 * 1024 * 1024


def _cparams(sem):
    return pltpu.CompilerParams(dimension_semantics=sem, vmem_limit_bytes=VMEM_LIMIT)


def _silu(x):
    return x * (1.0 / (1.0 + jnp.exp(-x)))


def _sigmoid(x):
    return 1.0 / (1.0 + jnp.exp(-x))


def _gelu(x):
    return 0.5 * x * (1.0 + jnp.tanh(0.7978845608028654 * (x + 0.044715 * (x * x * x))))


def _dot(a, b):
    return jnp.dot(a, b, preferred_element_type=F32)


def _dot_nt(a, b):
    return lax.dot_general(a, b, (((1,), (1,)), ((), ())), preferred_element_type=F32)


def _dot_tn(a, b):
    return lax.dot_general(a, b, (((0,), (0,)), ((), ())), preferred_element_type=F32)


def _mod_kernel(c_ref, w_ref, b_ref, o_ref):
    o_ref[...] = jnp.dot(_silu(c_ref[...]), w_ref[...], preferred_element_type=F32,
                         precision=lax.Precision.HIGHEST) + b_ref[...]


def _mod_call(cin, w_mod, b_mod):
    depth = w_mod.shape[0]
    tn = 1024
    return pl.pallas_call(
        _mod_kernel,
        grid=(depth, 6 * D // tn),
        in_specs=[pl.BlockSpec((8, D), lambda l, j: (0, 0)),
                  pl.BlockSpec((None, D, tn), lambda l, j: (l, 0, j)),
                  pl.BlockSpec((None, 1, tn), lambda l, j: (l, 0, j))],
        out_specs=pl.BlockSpec((None, 8, tn), lambda l, j: (l, 0, j)),
        out_shape=jax.ShapeDtypeStruct((depth, 8, 6 * D), F32),
        compiler_params=_cparams(("parallel", "parallel")),
        name="mod",
    )(cin, w_mod, b_mod.reshape(depth, 1, 6 * D))


def _inproj_kernel(x_ref, mod_ref, g_ref, w_ref, o_ref, hn_ref):
    @pl.when(pl.program_id(1) == 0)
    def _():
        x = x_ref[...]
        y = x * lax.rsqrt(jnp.mean(x * x, axis=-1, keepdims=True) + EPS) * g_ref[...]
        hn_ref[...] = (y * (1.0 + mod_ref[:, D:2 * D]) + mod_ref[:, 0:D]).astype(BF16)

    o_ref[...] = _dot(hn_ref[...], w_ref[...])


def _inproj_call(xs, mods, g, w, grp, tm=512, tn=512):
    r = xs.shape[0]
    return pl.pallas_call(
        _inproj_kernel,
        grid=(r // tm, N_PROJ // tn),
        in_specs=[pl.BlockSpec((tm, D), lambda i, j: (i, 0)),
                  pl.BlockSpec((None, 1, 6 * D), lambda i, j: (grp(i * tm), 0, 0)),
                  pl.BlockSpec((1, D), lambda i, j: (0, 0)),
                  pl.BlockSpec((D, tn), lambda i, j: (0, j))],
        out_specs=pl.BlockSpec((tm, tn), lambda i, j: (i, j)),
        out_shape=jax.ShapeDtypeStruct((r, N_PROJ), F32),
        scratch_shapes=[pltpu.VMEM((tm, D), BF16)],
        compiler_params=_cparams(("parallel", "arbitrary")),
        name="inproj",
    )(xs, mods, g, w)


def _gdn_prep_kernel(x_ref, prev_ref, next_ref, ba_ref, cw_ref, gp_ref,
                     q_ref, k_ref, v_ref, gb_ref, ext_ref, *, tp, bl, l, lc):
    i = pl.program_id(0)
    r0 = i * tp
    r1 = r0 + tp
    is_start = jnp.where(r0 < bl, r0 % l == 0, (r0 - bl) % lc == 0)
    is_end = jnp.where(r1 <= bl, r1 % l == 0, (r1 - bl) % lc == 0)
    ext_ref[0:8, :] = jnp.where(is_start, 0.0, prev_ref[...])
    ext_ref[8:8 + tp, :] = x_ref[...]
    ext_ref[8 + tp:16 + tp, :] = jnp.where(is_end, 0.0, next_ref[...])
    y = cw_ref[0:1, :] * ext_ref[6:6 + tp, :]
    for j in range(1, GDN_CONV):
        y = y + cw_ref[j:j + 1, :] * ext_ref[6 + j:6 + j + tp, :]
    y = _silu(y)
    for h in range(GDN_H):
        qh = y[:, h * GDN_DK:(h + 1) * GDN_DK]
        q_ref[:, h * GDN_DK:(h + 1) * GDN_DK] = qh * (
            lax.rsqrt(jnp.sum(qh * qh, axis=-1, keepdims=True) + EPS) * (GDN_DK ** -0.5))
        kh = y[:, HALF + h * GDN_DK:HALF + (h + 1) * GDN_DK]
        k_ref[:, h * GDN_DK:(h + 1) * GDN_DK] = kh * lax.rsqrt(jnp.sum(kh * kh, axis=-1, keepdims=True) + EPS)
    v_ref[...] = y[:, 2 * HALF:3 * HALF]
    ba = ba_ref[...]
    beta = _sigmoid(ba)
    t = ba + gp_ref[1:2, :]
    sp = jnp.maximum(t, 0.0) + jnp.log(1.0 + jnp.exp(-jnp.abs(t)))
    g = -jnp.exp(gp_ref[0:1, :]) * sp
    lane = lax.broadcasted_iota(jnp.int32, ba.shape, 1)
    gb_ref[...] = jnp.where(lane < 2 * GDN_H, beta, g)


def _gdn_prep_call(proj, conv_w, gparams, bl, l, lc, tp=256):
    r = proj.shape[0]
    nb8 = r // 8
    qkv_blk = COL_QKV // (3 * HALF)
    kern = functools.partial(_gdn_prep_kernel, tp=tp, bl=bl, l=l, lc=lc)
    out_sd = jax.ShapeDtypeStruct((r, HALF), F32)
    return pl.pallas_call(
        kern,
        grid=(r // tp,),
        in_specs=[pl.BlockSpec((tp, 3 * HALF), lambda i: (i, qkv_blk)),
                  pl.BlockSpec((8, 3 * HALF), lambda i: (jnp.maximum(i * (tp // 8) - 1, 0), qkv_blk)),
                  pl.BlockSpec((8, 3 * HALF), lambda i: (jnp.minimum((i + 1) * (tp // 8), nb8 - 1), qkv_blk)),
                  pl.BlockSpec((tp, 128), lambda i: (i, COL_BA // 128)),
                  pl.BlockSpec((GDN_CONV, 3 * HALF), lambda i: (0, 0)),
                  pl.BlockSpec((2, 128), lambda i: (0, 0))],
        out_specs=[pl.BlockSpec((tp, HALF), lambda i: (i, 0)),
                   pl.BlockSpec((tp, HALF), lambda i: (i, 0)),
                   pl.BlockSpec((tp, HALF), lambda i: (i, 0)),
                   pl.BlockSpec((tp, 128), lambda i: (i, 0))],
        out_shape=[out_sd, out_sd, out_sd, jax.ShapeDtypeStruct((r, 128), F32)],
        scratch_shapes=[pltpu.VMEM((tp + 16, 3 * HALF), F32)],
        compiler_params=_cparams(("parallel",)),
        name="gdn_prep",
    )(proj, proj, proj, proj, conv_w, gparams)


def _gdn_scan_kernel(qf, kf, vf, gf, qb, kb, vb, gb, of_ref, ob_ref, s_ref):
    @pl.when(pl.program_id(1) == 0)
    def _():
        s_ref[...] = jnp.zeros_like(s_ref)

    c = GDN_C
    row = lax.broadcasted_iota(jnp.int32, (c, c), 0)
    col = lax.broadcasted_iota(jnp.int32, (c, c), 1)
    eye = jnp.where(row == col, 1.0, 0.0)
    hp = lax.Precision.HIGHEST
    refs = ((qf, kf, vf, gf, of_ref), (qb, kb, vb, gb, ob_ref))
    chains = [(d, h) for d in range(2) for h in range(GDN_H)]
    masks, gcs = [], []
    for d in range(2):
        incl, strict = (row >= col, row > col) if d == 0 else (row <= col, row < col)
        masks.append((incl, strict))
        g = refs[d][3][...]
        tri = jnp.where(incl, 1.0, 0.0)
        tri_t = jnp.where(row <= col if d == 0 else row >= col, 1.0, 0.0)
        gcs.append((g, jnp.dot(tri, g, preferred_element_type=F32, precision=hp),
                    lax.dot_general(g, tri_t, (((0,), (0,)), ((), ())), preferred_element_type=F32,
                                    precision=hp)))

    st = {}
    for d, h in chains:
        sl = slice(h * GDN_DK, (h + 1) * GDN_DK)
        q, k, v = refs[d][0][:, sl], refs[d][1][:, sl], refs[d][2][:, sl]
        g, gc_col, gc_row = gcs[d]
        incl, strict = masks[d]
        cb, cg = d * GDN_H + h, 2 * GDN_H + d * GDN_H + h
        gcol, grow, beta = gc_col[:, cg:cg + 1], gc_row[cg:cg + 1, :], g[:, cb:cb + 1]
        last = c - 1 if d == 0 else 0
        glast = gc_col[last:last + 1, cg:cg + 1]
        decay = jnp.where(incl, jnp.exp(jnp.where(incl, gcol - grow, 0.0)), 0.0)
        eg = jnp.exp(gcol)
        kb_ = k.astype(BF16)
        s = s_ref[d, h]
        sb = s.astype(BF16)
        st[d, h] = dict(
            sl=sl, decay=decay, strict=strict, beta=beta, s=s, sb=sb, glast=glast,
            kk=_dot_nt(kb_, kb_), qk=_dot_nt(q.astype(BF16), kb_),
            qs=_dot((q * eg).astype(BF16), sb),
            rhs=jnp.concatenate([v * beta, k * (beta * eg)], axis=1).astype(BF16),
            kdec=(k * jnp.exp(glast - gcol)).astype(BF16))
    for ch in chains:
        t = st[ch]
        x = jnp.where(t["strict"], -(t["kk"] * t["decay"] * t["beta"]), 0.0)
        t["p"], t["tinv"] = x, eye + x
    for _ in range(5):
        for ch in chains:
            pb = st[ch]["p"].astype(BF16)
            st[ch]["p"] = _dot(pb, pb)
        for ch in chains:
            t = st[ch]
            t["tinv"] = t["tinv"] + _dot(t["tinv"].astype(BF16), t["p"].astype(BF16))
    for ch in chains:
        t = st[ch]
        t["uw"] = _dot(t["tinv"].astype(BF16), t["rhs"])
    for ch in chains:
        t = st[ch]
        t["vb"] = (t["uw"][:, :GDN_DV] - _dot(t["uw"][:, GDN_DV:].astype(BF16), t["sb"])).astype(BF16)
    for d, h in chains:
        t = st[d, h]
        refs[d][4][:, t["sl"]] = t["qs"] + _dot((t["qk"] * t["decay"]).astype(BF16), t["vb"])
        s_ref[d, h] = t["s"] * jnp.exp(t["glast"]) + _dot_tn(t["kdec"], t["vb"])


def _gdn_scan_call(q, k, v, gbm, b, l, lc):
    r = q.shape[0]
    c = GDN_C
    ncc, ncl = lc // c, l // c
    base = b * l // c

    def fwd(bi, s):
        return jnp.where(s < ncc, base + bi * ncc + s, bi * ncl + s - ncc)

    def bwd(bi, s):
        return jnp.where(s < ncc, base + bi * ncc + (ncc - 1 - s), bi * ncl + (ncl - 1) - (s - ncc))

    def specs(fn):
        return [pl.BlockSpec((c, HALF), lambda bi, s: (fn(bi, s), 0))] * 3 + [
            pl.BlockSpec((c, 128), lambda bi, s: (fn(bi, s), 0))]

    out_sd = jax.ShapeDtypeStruct((r, HALF), F32)
    return pl.pallas_call(
        _gdn_scan_kernel,
        grid=(b, ncc + ncl),
        in_specs=specs(fwd) + specs(bwd),
        out_specs=[pl.BlockSpec((c, HALF), lambda bi, s: (fwd(bi, s), 0)),
                   pl.BlockSpec((c, HALF), lambda bi, s: (bwd(bi, s), 0))],
        out_shape=[out_sd, out_sd],
        scratch_shapes=[pltpu.VMEM((2, GDN_H, GDN_DK, GDN_DV), F32)],
        compiler_params=_cparams(("parallel", "arbitrary")),
        name="gdn_scan",
    )(q, k, v, gbm, q, k, v, gbm)


def _da_prep_kernel(q_ref, k_ref, v_ref, cos_ref, sin_ref, gq_ref, gk_ref, qs_ref, kp_ref, vp_ref):
    cs, sn = cos_ref[...], sin_ref[...]
    lane = lax.broadcasted_iota(jnp.int32, cs.shape, 1)
    lo64 = lane < DA_DH
    lo32 = (lane % DA_DH) < (DA_DH // 2)

    def norm_rope(t, g):
        sq = t * t
        ms0 = jnp.sum(jnp.where(lo64, sq, 0.0), axis=-1, keepdims=True)
        ms1 = jnp.sum(jnp.where(lo64, 0.0, sq), axis=-1, keepdims=True)
        inv = lax.rsqrt(jnp.where(lo64, ms0, ms1) * (1.0 / DA_DH) + EPS)
        t = t * inv * g
        swapped = jnp.where(lo32, pltpu.roll(t, 128 - DA_DH // 2, 1), pltpu.roll(t, DA_DH // 2, 1))
        return t * cs + swapped * sn

    for h in range(DA_H):
        sl = slice(h * 128, (h + 1) * 128)
        qh = norm_rope(q_ref[:, sl], gq_ref[...]) * (DA_DH ** -0.5 * LOG2E)
        qs_ref[h, 0] = jnp.where(lo64, qh, 0.0).astype(BF16)
        qs_ref[h, 1] = jnp.where(lo64, 0.0, qh).astype(BF16)
        kp_ref[h] = norm_rope(k_ref[:, sl], gk_ref[...]).astype(BF16)
        vp_ref[h] = v_ref[:, sl].T.astype(BF16)


def _da_prep_call(proj, cos_t, sin_t, gq, gk, bl, l, tp=256):
    r = proj.shape[0]
    ntl = l // tp

    def pos(i):
        return jnp.where(i * tp < bl, i % ntl, ntl)

    return pl.pallas_call(
        _da_prep_kernel,
        grid=(r // tp,),
        in_specs=[pl.BlockSpec((tp, HALF), lambda i: (i, COL_Q // HALF)),
                  pl.BlockSpec((tp, HALF), lambda i: (i, COL_K // HALF)),
                  pl.BlockSpec((tp, HALF), lambda i: (i, COL_V // HALF)),
                  pl.BlockSpec((tp, 128), lambda i: (pos(i), 0)),
                  pl.BlockSpec((tp, 128), lambda i: (pos(i), 0)),
                  pl.BlockSpec((1, 128), lambda i: (0, 0)),
                  pl.BlockSpec((1, 128), lambda i: (0, 0))],
        out_specs=[pl.BlockSpec((DA_H, 2, tp, 128), lambda i: (0, 0, i, 0)),
                   pl.BlockSpec((DA_H, tp, 128), lambda i: (0, i, 0)),
                   pl.BlockSpec((DA_H, DA_DV, tp), lambda i: (0, 0, i))],
        out_shape=[jax.ShapeDtypeStruct((DA_H, 2, r, 128), BF16),
                   jax.ShapeDtypeStruct((DA_H, r, 128), BF16),
                   jax.ShapeDtypeStruct((DA_H, DA_DV, r), BF16)],
        compiler_params=_cparams(("parallel",)),
        name="da_prep",
    )(proj, proj, proj, cos_t, sin_t, gq, gk)


def _attn_kernel(q_ref, kc_ref, vc_ref, kl_ref, vl_ref, lam_ref, sg_ref, o_ref, m_ref, l_ref, acc_ref,
                 s0_ref, s1_ref, p0_ref, p1_ref, cm_ref, al_ref, *, tq, tkc, n_lat_tiles, lam_init):
    q = q_ref[...].reshape(2 * tq, 128)

    st = _dot_nt(kc_ref[...], q)
    m0 = jnp.max(st, axis=0, keepdims=True)
    p = jnp.exp2(st - m0)
    m_ref[...] = m0
    l_ref[...] = jnp.sum(p, axis=0, keepdims=True)
    acc_ref[...] = _dot(vc_ref[...], p.astype(BF16))

    def scores(j, s_ref, slot):
        off = pl.multiple_of(j * tkc, tkc)
        st = _dot_nt(kl_ref[pl.ds(off, tkc), :], q)
        s_ref[...] = st
        cm_ref[slot:slot + 1, :] = jnp.max(st, axis=0, keepdims=True)

    def softmax(s_ref, p_ref, slot):
        m_old = m_ref[...]
        m_new = jnp.maximum(m_old, cm_ref[slot:slot + 1, :])
        alpha = jnp.exp2(m_old - m_new)
        p = jnp.exp2(s_ref[...] - m_new)
        l_ref[...] = alpha * l_ref[...] + jnp.sum(p, axis=0, keepdims=True)
        p_ref[...] = p.astype(BF16)
        al_ref[slot:slot + 1, :] = alpha
        m_ref[...] = m_new

    def weighted_values(j, p_ref, slot):
        off = pl.multiple_of(j * tkc, tkc)
        acc_ref[...] = al_ref[slot:slot + 1, :] * acc_ref[...] + _dot(vl_ref[:, pl.ds(off, tkc)], p_ref[...])

    @pl.when(pl.program_id(2) < n_lat_tiles)
    def _():
        n = kl_ref.shape[0] // tkc
        scores(0, s0_ref, 0)
        p1_ref[...] = jnp.zeros_like(p1_ref)
        al_ref[1:2, :] = jnp.ones((1, 2 * tq), F32)

        def body(jj, carry):
            j = 2 * jj
            scores(j + 1, s1_ref, 1)
            softmax(s0_ref, p0_ref, 0)
            weighted_values(jnp.maximum(j - 1, 0), p1_ref, 1)
            scores(jnp.minimum(j + 2, n - 1), s0_ref, 0)
            softmax(s1_ref, p1_ref, 1)
            weighted_values(j, p0_ref, 0)
            return carry

        lax.fori_loop(0, n // 2, body, 0)
        weighted_values(n - 1, p1_ref, 1)

    lp = lam_ref[...]
    lam = (jnp.exp(jnp.sum(lp[0:1] * lp[1:2], axis=-1, keepdims=True))
           - jnp.exp(jnp.sum(lp[2:3] * lp[3:4], axis=-1, keepdims=True)) + lam_init)
    ot = acc_ref[...] * (1.0 / l_ref[...])
    ot = ot[:, :tq] - lam * ot[:, tq:]
    ot = ot * lax.rsqrt(jnp.mean(ot * ot, axis=0, keepdims=True) + EPS) * sg_ref[...]
    o_ref[...] = (ot * (1.0 - lam_init)).T


def _attn_call(qs, kp, vpt, lam_p, subln_g, b, l, lc, with_ctx_queries, lam_init, tq=256, tkc=1024):
    r = kp.shape[1]
    assert tq == lc and l % (2 * tkc) == 0
    ntl = l // tq
    nq = ntl + (1 if with_ctx_queries else 0)
    r_out = r if with_ctx_queries else b * l

    def qrow(bi, qi):
        return jnp.where(qi < ntl, bi * ntl + qi, b * ntl + bi)

    kern = functools.partial(_attn_kernel, tq=tq, tkc=tkc, n_lat_tiles=ntl, lam_init=lam_init)
    return pl.pallas_call(
        kern,
        grid=(b, DA_H, nq),
        in_specs=[pl.BlockSpec((None, 2, tq, 128), lambda bi, h, qi: (h, 0, qrow(bi, qi), 0)),
                  pl.BlockSpec((None, lc, 128), lambda bi, h, qi: (h, b * l // lc + bi, 0)),
                  pl.BlockSpec((None, DA_DV, lc), lambda bi, h, qi: (h, 0, b * l // lc + bi)),
                  pl.BlockSpec((None, l, 128), lambda bi, h, qi: (h, bi, 0)),
                  pl.BlockSpec((None, DA_DV, l), lambda bi, h, qi: (h, 0, bi)),
                  pl.BlockSpec((4, DA_DH), lambda bi, h, qi: (0, 0)),
                  pl.BlockSpec((DA_DV, 1), lambda bi, h, qi: (0, 0))],
        out_specs=pl.BlockSpec((tq, 128), lambda bi, h, qi: (qrow(bi, qi), h)),
        out_shape=jax.ShapeDtypeStruct((r_out, HALF), F32),
        scratch_shapes=[pltpu.VMEM((1, 2 * tq), F32), pltpu.VMEM((1, 2 * tq), F32), pltpu.VMEM((DA_DV, 2 * tq), F32),
                        pltpu.VMEM((tkc, 2 * tq), F32), pltpu.VMEM((tkc, 2 * tq), F32),
                        pltpu.VMEM((tkc, 2 * tq), BF16), pltpu.VMEM((tkc, 2 * tq), BF16),
                        pltpu.VMEM((8, 2 * tq), F32), pltpu.VMEM((8, 2 * tq), F32)],
        compiler_params=_cparams(("parallel", "parallel", "arbitrary")),
        name="attn",
    )(qs, kp, vpt, kp, vpt, lam_p, subln_g)


def _merge_kernel(of_ref, ob_ref, z_ref, uv_ref, yc_ref, gates_ref, x_ref, mod_ref, gn_ref, vn_ref, ws_ref, bs_ref,
                  wa_ref, wb_ref, wc_ref, wo_ref, o_ref, yb_ref, *, tm):
    oa = of_ref[...] + ob_ref[...]
    z = z_ref[...]
    parts = []
    for h in range(GDN_H):
        oh = oa[:, h * GDN_DV:(h + 1) * GDN_DV]
        parts.append(oh * lax.rsqrt(jnp.mean(oh * oh, axis=-1, keepdims=True) + EPS) * gn_ref[...])
    ya = jnp.concatenate(parts, axis=1) * _silu(z)
    uv = _gelu(uv_ref[...])
    u, v = uv[:, :HALF], uv[:, HALF:]
    v = (v * lax.rsqrt(jnp.mean(v * v, axis=-1, keepdims=True) + EPS) * vn_ref[...]).astype(BF16)
    for ck in range(tm // CMLP_CHUNK):
        rs = slice(ck * CMLP_CHUNK, (ck + 1) * CMLP_CHUNK)
        for g in range(CMLP_G):
            cs = slice(g * CMLP_CH, (g + 1) * CMLP_CH)
            sv = _dot(ws_ref[g], v[rs, cs]) + bs_ref[:, g:g + 1]
            yb_ref[rs, cs] = u[rs, cs] * sv
    yb = yb_ref[...]
    gts = _sigmoid(gates_ref[...])
    y = (gts[:, 0:D] * _dot(ya.astype(BF16), wa_ref[...])
         + gts[:, D:2 * D] * _dot(yb.astype(BF16), wb_ref[...])
         + gts[:, 2 * D:3 * D] * _dot(yc_ref[...].astype(BF16), wc_ref[...]))
    mix = _dot(y.astype(BF16), wo_ref[...])
    o_ref[...] = x_ref[...] + mod_ref[:, 2 * D:3 * D] * mix


def _merge_call(o_f, o_b, proj, yc, xs, mods, gn, vn, ws, bs_t, wa, wb, wc, wo, grp, r_out, tm=256):
    kern = functools.partial(_merge_kernel, tm=tm)
    full = lambda shape: pl.BlockSpec(shape, lambda i: (0,) * len(shape))
    return pl.pallas_call(
        kern,
        grid=(r_out // tm,),
        in_specs=[pl.BlockSpec((tm, HALF), lambda i: (i, 0)),
                  pl.BlockSpec((tm, HALF), lambda i: (i, 0)),
                  pl.BlockSpec((tm, HALF), lambda i: (i, COL_Z // HALF)),
                  pl.BlockSpec((tm, 2 * HALF), lambda i: (i, COL_UV // (2 * HALF))),
                  pl.BlockSpec((tm, HALF), lambda i: (i, 0)),
                  pl.BlockSpec((tm, 3 * D), lambda i: (i, 0)),
                  pl.BlockSpec((tm, D), lambda i: (i, 0)),
                  pl.BlockSpec((None, 1, 6 * D), lambda i: (grp(i * tm), 0, 0)),
                  full((1, GDN_DV)), full((1, HALF)), full((CMLP_G, CMLP_CHUNK, CMLP_CHUNK)),
                  full((CMLP_CHUNK, CMLP_G)),
                  full((HALF, D)), full((HALF, D)), full((HALF, D)), full((D, D))],
        out_specs=pl.BlockSpec((tm, D), lambda i: (i, 0)),
        out_shape=jax.ShapeDtypeStruct((r_out, D), F32),
        scratch_shapes=[pltpu.VMEM((tm, HALF), F32)],
        compiler_params=_cparams(("parallel",)),
        name="merge",
    )(o_f, o_b, proj, proj, yc, proj, xs, mods, gn, vn, ws, bs_t, wa, wb, wc, wo)


def _top_values(s, n):
    return _top_ranked(s, n)[0]


def _top_ranked(s, n):
    rid = lax.broadcasted_iota(jnp.int32, (n, s.shape[1]), 0)
    out = jnp.zeros((n, s.shape[1]), F32)
    rank = jnp.full(s.shape, float(n), F32)
    for k in range(n):
        m = jnp.max(s, axis=0, keepdims=True)
        out = jnp.where(rid == k, m, out)
        hit = s == m
        rank = jnp.where(hit, float(k), rank)
        s = jnp.where(hit, NEG, s)
    return out, rank


def _dup_bf16(x):
    bits = pltpu.bitcast(x, jnp.uint32)
    hi = (bits + jnp.uint32(0x7FFF) + ((bits >> 16) & jnp.uint32(1))) >> 16
    return pltpu.bitcast((hi << 16) | hi, F32)


def _peer_sel_kernel(x_ref, mod_ref, g_ref, wq_ref, sk_ref, hn_ref, n1_ref, r2_ref, e1_ref, e2_ref):
    x = x_ref[...]
    y = x * lax.rsqrt(jnp.mean(x * x, axis=-1, keepdims=True) + EPS) * g_ref[...]
    hn = (y * (1.0 + mod_ref[:, 4 * D:5 * D]) + mod_ref[:, 3 * D:4 * D]).astype(BF16)
    hn_ref[...] = hn
    q = _dot(hn, wq_ref[...]).astype(BF16)
    for h in range(PEER_H):
        half = PEER_QD // 2
        s1 = _dot_nt(sk_ref[h, 0], q[:, (2 * h) * half:(2 * h + 1) * half])
        s2 = _dot_nt(sk_ref[h, 1], q[:, (2 * h + 1) * half:(2 * h + 2) * half])
        v1 = _top_values(s1, PEER_K)
        v2, rank2 = _top_ranked(s2, PEER_K)
        cand = jnp.concatenate([v1[k:k + 1] + v2 for k in range(PEER_K)], axis=0)
        best = _top_values(cand, PEER_K)
        zsum = jnp.sum(jnp.exp(best - best[0:1]), axis=0, keepdims=True)
        thr = best[PEER_K - 1:PEER_K]
        n1 = jnp.zeros_like(s1)
        for k in range(PEER_K):
            n1 = n1 + jnp.where(s1 + v2[k:k + 1] >= thr, 1.0, 0.0)
        n1_ref[h] = _dup_bf16(n1)
        r2_ref[h] = pltpu.bitcast(rank2.astype(BF16), F32)
        e1_ref[h] = _dup_bf16(jnp.exp(s1 - v1[0:1]) * (0.5 / zsum))
        e2_ref[h] = pltpu.bitcast(jnp.exp(s2 - v2[0:1]).astype(BF16), F32)


def _peer_sel_call(xs, mods, g, wq, sk, grp, r_act, ts=256):
    full = lambda shape: pl.BlockSpec(shape, lambda i: (0,) * len(shape))
    sc_spec = pl.BlockSpec((PEER_H, PEER_NK, ts), lambda i: (0, 0, i))
    pk_spec = pl.BlockSpec((PEER_H, PEER_NK // 2, ts), lambda i: (0, 0, i))
    sd32 = jax.ShapeDtypeStruct((PEER_H, PEER_NK, r_act), F32)
    sd16 = jax.ShapeDtypeStruct((PEER_H, PEER_NK // 2, r_act), F32)
    return pl.pallas_call(
        _peer_sel_kernel,
        grid=(r_act // ts,),
        in_specs=[pl.BlockSpec((ts, D), lambda i: (i, 0)),
                  pl.BlockSpec((None, 1, 6 * D), lambda i: (grp(i * ts), 0, 0)),
                  full((1, D)), full((D, PEER_H * PEER_QD)), full((PEER_H, 2, PEER_NK, PEER_QD // 2))],
        out_specs=[pl.BlockSpec((ts, D), lambda i: (i, 0)), sc_spec, pk_spec, sc_spec, pk_spec],
        out_shape=[jax.ShapeDtypeStruct((r_act, D), BF16), sd32, sd16, sd32, sd16],
        compiler_params=_cparams(("parallel",)),
        name="peer_sel",
    )(xs, mods, g, wq, sk)


def _peer_dense_kernel(hn_ref, u_ref, v_ref, n1_ref, r2_ref, e1_ref, e2_ref, x_ref, mod_ref, o_ref, acc_ref, g_ref,
                       *, te):
    e = pl.program_id(1)
    nr = te // PEER_NK
    tmh = hn_ref.shape[0] // 2

    def rows16(ref, h, e1, ls):
        return pltpu.bitcast(jnp.broadcast_to(ref[h, pl.ds(e1, 1), ls], (PEER_NK // 2, tmh)), BF16)

    def store_gates(et):
        for c in range(2):
            ls = slice(c * tmh, (c + 1) * tmh)
            for r in range(nr):
                e1 = jnp.minimum(et * nr + r, PEER_NK - 1)
                gsum = None
                for h in range(PEER_H):
                    term = jnp.where(pltpu.bitcast(r2_ref[h, :, ls], BF16) < rows16(n1_ref, h, e1, ls),
                                     rows16(e1_ref, h, e1, ls) * pltpu.bitcast(e2_ref[h, :, ls], BF16),
                                     jnp.zeros((), BF16))
                    gsum = term if gsum is None else gsum + term
                g_ref[r, :, ls] = pltpu.bitcast(gsum, F32)

    @pl.when(e == 0)
    def _():
        acc_ref[...] = jnp.zeros_like(acc_ref)
        store_gates(0)

    def weights(a, c):
        act = a + a * jnp.tanh(a * (0.7978845608028654 + (0.7978845608028654 * 0.044715) * (a * a)))
        return jnp.concatenate(
            [pltpu.bitcast(g_ref[r, :, c * tmh:(c + 1) * tmh], BF16) * act[r * PEER_NK:(r + 1) * PEER_NK].astype(BF16)
             for r in range(nr)], axis=0)

    a0 = _dot_nt(u_ref[...], hn_ref[0:tmh, :])
    a1 = _dot_nt(u_ref[...], hn_ref[tmh:2 * tmh, :])
    w0 = weights(a0, 0)
    acc_ref[0:tmh, :] += _dot_tn(w0, v_ref[...])
    w1 = weights(a1, 1)
    acc_ref[tmh:2 * tmh, :] += _dot_tn(w1, v_ref[...])
    store_gates(e + 1)

    @pl.when(e == pl.num_programs(1) - 1)
    def _():
        o_ref[...] = x_ref[...] + mod_ref[:, 5 * D:6 * D] * acc_ref[...]


def _peer_dense_call(hn, u_tab, v_tab, n1, r2, e1, e2, xs, mods, grp, r_act, tm=512, te=512):
    kern = functools.partial(_peer_dense_kernel, te=te)
    sc_spec = pl.BlockSpec((PEER_H, PEER_NK, tm), lambda i, e: (0, 0, i))
    pk_spec = pl.BlockSpec((PEER_H, PEER_NK // 2, tm), lambda i, e: (0, 0, i))
    return pl.pallas_call(
        kern,
        grid=(r_act // tm, PEER_E // te),
        in_specs=[pl.BlockSpec((tm, D), lambda i, e: (i, 0)),
                  pl.BlockSpec((te, D), lambda i, e: (e, 0)),
                  pl.BlockSpec((te, D), lambda i, e: (e, 0)),
                  sc_spec, pk_spec, sc_spec, pk_spec,
                  pl.BlockSpec((tm, D), lambda i, e: (i, 0)),
                  pl.BlockSpec((None, 1, 6 * D), lambda i, e: (grp(i * tm), 0, 0))],
        out_specs=pl.BlockSpec((tm, D), lambda i, e: (i, 0)),
        out_shape=jax.ShapeDtypeStruct((r_act, D), F32),
        scratch_shapes=[pltpu.VMEM((tm, D), F32), pltpu.VMEM((te // PEER_NK, PEER_NK // 2, tm), F32)],
        compiler_params=_cparams(("parallel", "arbitrary")),
        name="peer_dense",
    )(hn, u_tab, v_tab, n1, r2, e1, e2, xs, mods)


def _rope_tables(l, lc):
    rows = l // GRID_W
    r, col = jnp.meshgrid(jnp.arange(rows), jnp.arange(GRID_W), indexing='ij')
    n_freq = DA_DH // 4
    inv = ROPE_BASE ** (-jnp.arange(n_freq, dtype=F32) / n_freq)
    ang = jnp.concatenate([r.reshape(-1, 1).astype(F32) * inv, col.reshape(-1, 1).astype(F32) * inv], axis=-1)
    cos, sin = jnp.cos(ang), jnp.sin(ang)
    cos_t = jnp.concatenate([jnp.tile(cos, (1, 4)), jnp.ones((lc, 128), F32)], axis=0)
    sin_t = jnp.concatenate([jnp.tile(jnp.concatenate([-sin, sin], axis=1), (1, 2)), jnp.zeros((lc, 128), F32)], axis=0)
    return cos_t, sin_t


def kernel(x, c, ctx, c_ctx, w_mod, b_mod, norm1_g, w_in, gdn_conv_w, gdn_a_log, gdn_dt_bias, gdn_out_norm_g,
           cmlp_v_norm_g, cmlp_w_s, cmlp_b_s, da_q_norm_g, da_k_norm_g, da_lambda, da_subln_g, w_branch_a,
           w_branch_b, w_branch_c, w_out, norm2_g, peer_w_q, peer_subkeys, peer_u, peer_v):
    b, l, d = x.shape
    lc = ctx.shape[1]
    depth = w_mod.shape[0]
    bl = b * l
    r_all = bl + b * lc
    assert d == D and l % 512 == 0 and lc == 256 and bl % 512 == 0 and (b * lc) % 512 == 0

    def grp(row):
        return jnp.where(row < bl, row // l, b)

    xs = jnp.concatenate([x.reshape(bl, d), ctx.reshape(b * lc, d)], axis=0)
    cin = jnp.concatenate([c, c_ctx[None, :], jnp.zeros((8 - b - 1, d), F32)], axis=0)
    mods_all = _mod_call(cin, w_mod, b_mod)[:, :b + 1].reshape(depth, b + 1, 1, 6 * d)
    cos_t, sin_t = _rope_tables(l, lc)

    for li in range(depth):
        last = li == depth - 1
        lam_init = 0.8 - 0.6 * math.exp(-0.3 * li)
        mods = mods_all[li]
        w = w_in[li]
        w_re = jnp.concatenate(
            [w[:, 4624:7696], w[:, 0:1536], w[:, 1536:2048], w[:, 2064:3088], w[:, 3088:3600], w[:, 3600:4112],
             w[:, 4112:4624], w[:, 2048:2064], jnp.zeros((d, N_PROJ - COL_BA - 16), F32)], axis=1).astype(BF16)
        proj = _inproj_call(xs, mods, norm1_g[li][None, :], w_re, grp)

        gparams = jnp.zeros((2, 128), F32)
        gparams = gparams.at[0, 8:16].set(gdn_a_log[li].reshape(-1)).at[1, 8:16].set(gdn_dt_bias[li].reshape(-1))
        gq, gk, gv, gbm = _gdn_prep_call(proj, gdn_conv_w[li], gparams, bl, l, lc)
        o_f, o_b = _gdn_scan_call(gq, gk, gv, gbm, b, l, lc)

        gq2 = jnp.tile(da_q_norm_g[li], 2)[None, :]
        gk2 = jnp.tile(da_k_norm_g[li], 2)[None, :]
        qs, kp, vp = _da_prep_call(proj, cos_t, sin_t, gq2, gk2, bl, l)
        yc = _attn_call(qs, kp, vp, da_lambda[li], da_subln_g[li][:, None], b, l, lc, not last, lam_init)

        r_act = bl if last else r_all
        xs = _merge_call(o_f, o_b, proj, yc, xs, mods, gdn_out_norm_g[li][None, :], cmlp_v_norm_g[li][None, :],
                         cmlp_w_s[li].astype(BF16), cmlp_b_s[li].T, w_branch_a[li].astype(BF16),
                         w_branch_b[li].astype(BF16), w_branch_c[li].astype(BF16), w_out[li].astype(BF16), grp, r_act)

        hn, n1, r2, e1, e2 = _peer_sel_call(xs, mods, norm2_g[li][None, :], peer_w_q[li].astype(BF16),
                                            peer_subkeys[li].astype(BF16), grp, r_act)
        xs = _peer_dense_call(hn, peer_u[li].astype(BF16), peer_v[li].astype(BF16), n1, r2, e1, e2, xs, mods,
                              grp, r_act)
    return xs.reshape(b, l, d)
```

```python
import functools
import math

import jax
import jax.numpy as jnp
from jax import lax
from jax.experimental import pallas as pl
from jax.experimental.pallas import tpu as pltpu

F32 = jnp.float32
BF16 = jnp.bfloat16

D = 1024
EPS = 1e-6
GRID_W = 64
GDN_H, GDN_DK, GDN_DV, GDN_CONV, GDN_C = 4, 128, 128, 5, 64
CMLP_G, CMLP_CH, CMLP_CHUNK = 4, 128, 128
DA_H, DA_DH, DA_DV = 4, 64, 128
ROPE_BASE = 10000.0
PEER_H, PEER_NK, PEER_QD, PEER_K = 8, 128, 256, 16
PEER_E = PEER_NK * PEER_NK
HALF = GDN_H * GDN_DV

COL_GATES, COL_QKV, COL_Z, COL_UV, COL_Q, COL_K, COL_V, COL_BA = 0, 3072, 4608, 5120, 6144, 6656, 7168, 7680
N_PROJ = COL_BA + 128

NEG = -1e30
LOG2E = 1.4426950408889634
VMEM_LIMIT = 56 * 1024 * 1024


def _cparams(sem, flags=None):
    return pltpu.CompilerParams(dimension_semantics=sem, vmem_limit_bytes=VMEM_LIMIT, flags=flags)


def _silu(x):
    return x * (1.0 / (1.0 + jnp.exp(-x)))


def _sigmoid(x):
    return 1.0 / (1.0 + jnp.exp(-x))


def _gelu(x):
    return 0.5 * x * (1.0 + jnp.tanh(0.7978845608028654 * (x + 0.044715 * (x * x * x))))


def _dot(a, b):
    return jnp.dot(a, b, preferred_element_type=F32)


def _dot_nt(a, b):
    return lax.dot_general(a, b, (((1,), (1,)), ((), ())), preferred_element_type=F32)


def _dot_tn(a, b):
    return lax.dot_general(a, b, (((0,), (0,)), ((), ())), preferred_element_type=F32)


def _mod_kernel(c_ref, w_ref, b_ref, o_ref):
    o_ref[...] = jnp.dot(_silu(c_ref[...]), w_ref[...], preferred_element_type=F32,
                         precision=lax.Precision.HIGHEST) + b_ref[...]


def _mod_call(cin, w_mod, b_mod):
    depth = w_mod.shape[0]
    tn = 1024
    return pl.pallas_call(
        _mod_kernel,
        grid=(depth, 6 * D // tn),
        in_specs=[pl.BlockSpec((8, D), lambda l, j: (0, 0)),
                  pl.BlockSpec((None, D, tn), lambda l, j: (l, 0, j)),
                  pl.BlockSpec((None, 1, tn), lambda l, j: (l, 0, j))],
        out_specs=pl.BlockSpec((None, 8, tn), lambda l, j: (l, 0, j)),
        out_shape=jax.ShapeDtypeStruct((depth, 8, 6 * D), F32),
        compiler_params=_cparams(("parallel", "parallel")),
        name="mod",
    )(cin, w_mod, b_mod.reshape(depth, 1, 6 * D))


def _inproj_kernel(x_ref, mod_ref, g_ref, w_ref, o_ref, *, tn):
    x = x_ref[...]
    y = x * lax.rsqrt(jnp.mean(x * x, axis=-1, keepdims=True) + EPS) * g_ref[...]
    hn = (y * (1.0 + mod_ref[:, D:2 * D]) + mod_ref[:, 0:D]).astype(BF16)
    for c0 in range(0, N_PROJ, tn):
        c1 = min(c0 + tn, N_PROJ)
        o_ref[:, c0:c1] = _dot(hn, w_ref[:, c0:c1])


def _inproj_call(xs, mods, g, w, grp, tm=256, tn=1024):
    r = xs.shape[0]
    return pl.pallas_call(
        functools.partial(_inproj_kernel, tn=tn),
        grid=(r // tm,),
        in_specs=[pl.BlockSpec((tm, D), lambda i: (i, 0)),
                  pl.BlockSpec((None, 1, 6 * D), lambda i: (grp(i * tm), 0, 0)),
                  pl.BlockSpec((1, D), lambda i: (0, 0)),
                  pl.BlockSpec((D, N_PROJ), lambda i: (0, 0), pipeline_mode=pl.Buffered(1))],
        out_specs=pl.BlockSpec((tm, N_PROJ), lambda i: (i, 0)),
        out_shape=jax.ShapeDtypeStruct((r, N_PROJ), F32),
        compiler_params=_cparams(("parallel",)),
        name="inproj",
    )(xs, mods, g, w)


def _gdn_prep_kernel(x_ref, prev_ref, next_ref, ba_ref, cw_ref, gp_ref,
                     q_ref, k_ref, v_ref, gb_ref, ext_ref, *, tp, bl, l, lc):
    i = pl.program_id(0)
    r0 = i * tp
    r1 = r0 + tp
    is_start = jnp.where(r0 < bl, r0 % l == 0, (r0 - bl) % lc == 0)
    is_end = jnp.where(r1 <= bl, r1 % l == 0, (r1 - bl) % lc == 0)
    ext_ref[0:8, :] = jnp.where(is_start, 0.0, prev_ref[...])
    ext_ref[8:8 + tp, :] = x_ref[...]
    ext_ref[8 + tp:16 + tp, :] = jnp.where(is_end, 0.0, next_ref[...])
    y = cw_ref[0:1, :] * ext_ref[6:6 + tp, :]
    for j in range(1, GDN_CONV):
        y = y + cw_ref[j:j + 1, :] * ext_ref[6 + j:6 + j + tp, :]
    y = _silu(y)
    for h in range(GDN_H):
        qh = y[:, h * GDN_DK:(h + 1) * GDN_DK]
        q_ref[:, h * GDN_DK:(h + 1) * GDN_DK] = qh * (
            lax.rsqrt(jnp.sum(qh * qh, axis=-1, keepdims=True) + EPS) * (GDN_DK ** -0.5))
        kh = y[:, HALF + h * GDN_DK:HALF + (h + 1) * GDN_DK]
        k_ref[:, h * GDN_DK:(h + 1) * GDN_DK] = kh * lax.rsqrt(jnp.sum(kh * kh, axis=-1, keepdims=True) + EPS)
    v_ref[...] = y[:, 2 * HALF:3 * HALF]
    ba = ba_ref[...]
    beta = _sigmoid(ba)
    t = ba + gp_ref[1:2, :]
    sp = jnp.maximum(t, 0.0) + jnp.log(1.0 + jnp.exp(-jnp.abs(t)))
    g = -jnp.exp(gp_ref[0:1, :]) * sp
    lane = lax.broadcasted_iota(jnp.int32, ba.shape, 1)
    gb_ref[...] = jnp.where(lane < 2 * GDN_H, beta, g)


def _gdn_prep_call(proj, conv_w, gparams, bl, l, lc, tp=256):
    r = proj.shape[0]
    nb8 = r // 8
    qkv_blk = COL_QKV // (3 * HALF)
    kern = functools.partial(_gdn_prep_kernel, tp=tp, bl=bl, l=l, lc=lc)
    out_sd = jax.ShapeDtypeStruct((r, HALF), F32)
    return pl.pallas_call(
        kern,
        grid=(r // tp,),
        in_specs=[pl.BlockSpec((tp, 3 * HALF), lambda i: (i, qkv_blk)),
                  pl.BlockSpec((8, 3 * HALF), lambda i: (jnp.maximum(i * (tp // 8) - 1, 0), qkv_blk)),
                  pl.BlockSpec((8, 3 * HALF), lambda i: (jnp.minimum((i + 1) * (tp // 8), nb8 - 1), qkv_blk)),
                  pl.BlockSpec((tp, 128), lambda i: (i, COL_BA // 128)),
                  pl.BlockSpec((GDN_CONV, 3 * HALF), lambda i: (0, 0)),
                  pl.BlockSpec((2, 128), lambda i: (0, 0))],
        out_specs=[pl.BlockSpec((tp, HALF), lambda i: (i, 0)),
                   pl.BlockSpec((tp, HALF), lambda i: (i, 0)),
                   pl.BlockSpec((tp, HALF), lambda i: (i, 0)),
                   pl.BlockSpec((tp, 128), lambda i: (i, 0))],
        out_shape=[out_sd, out_sd, out_sd, jax.ShapeDtypeStruct((r, 128), F32)],
        scratch_shapes=[pltpu.VMEM((tp + 16, 3 * HALF), F32)],
        compiler_params=_cparams(("parallel",)),
        name="gdn_prep",
    )(proj, proj, proj, proj, conv_w, gparams)


def _gdn_scan_kernel(qf, kf, vf, gf, qb, kb, vb, gb, of_ref, ob_ref, s_ref):
    @pl.when(pl.program_id(1) == 0)
    def _():
        s_ref[...] = jnp.zeros_like(s_ref)

    c = GDN_C
    row = lax.broadcasted_iota(jnp.int32, (c, c), 0)
    col = lax.broadcasted_iota(jnp.int32, (c, c), 1)
    eye = jnp.where(row == col, 1.0, 0.0)
    hp = lax.Precision.HIGHEST
    refs = ((qf, kf, vf, gf, of_ref), (qb, kb, vb, gb, ob_ref))
    chains = [(d, h) for d in range(2) for h in range(GDN_H)]
    masks, gcs = [], []
    for d in range(2):
        incl, strict = (row >= col, row > col) if d == 0 else (row <= col, row < col)
        masks.append((incl, strict))
        g = refs[d][3][...]
        tri = jnp.where(incl, 1.0, 0.0)
        tri_t = jnp.where(row <= col if d == 0 else row >= col, 1.0, 0.0)
        gcs.append((g, jnp.dot(tri, g, preferred_element_type=F32, precision=hp),
                    lax.dot_general(g, tri_t, (((0,), (0,)), ((), ())), preferred_element_type=F32,
                                    precision=hp)))

    st = {}
    for d, h in chains:
        sl = slice(h * GDN_DK, (h + 1) * GDN_DK)
        q, k, v = refs[d][0][:, sl], refs[d][1][:, sl], refs[d][2][:, sl]
        g, gc_col, gc_row = gcs[d]
        incl, strict = masks[d]
        cb, cg = d * GDN_H + h, 2 * GDN_H + d * GDN_H + h
        gcol, grow, beta = gc_col[:, cg:cg + 1], gc_row[cg:cg + 1, :], g[:, cb:cb + 1]
        last = c - 1 if d == 0 else 0
        glast = gc_col[last:last + 1, cg:cg + 1]
        decay = jnp.where(incl, jnp.exp(jnp.where(incl, gcol - grow, 0.0)), 0.0)
        eg = jnp.exp(gcol)
        kb_ = k.astype(BF16)
        s = s_ref[d, h]
        sb = s.astype(BF16)
        st[d, h] = dict(
            sl=sl, decay=decay, strict=strict, beta=beta, s=s, sb=sb, glast=glast,
            kk=_dot_nt(kb_, kb_), qk=_dot_nt(q.astype(BF16), kb_),
            qs=_dot((q * eg).astype(BF16), sb),
            rhs=jnp.concatenate([v * beta, k * (beta * eg)], axis=1).astype(BF16),
            kdec=(k * jnp.exp(glast - gcol)).astype(BF16))
    for ch in chains:
        t = st[ch]
        x = jnp.where(t["strict"], -(t["kk"] * t["decay"] * t["beta"]), 0.0)
        t["p"], t["tinv"] = x, eye + x
    for _ in range(5):
        for ch in chains:
            pb = st[ch]["p"].astype(BF16)
            st[ch]["p"] = _dot(pb, pb)
        for ch in chains:
            t = st[ch]
            t["tinv"] = t["tinv"] + _dot(t["tinv"].astype(BF16), t["p"].astype(BF16))
    for ch in chains:
        t = st[ch]
        t["uw"] = _dot(t["tinv"].astype(BF16), t["rhs"])
    for ch in chains:
        t = st[ch]
        t["vb"] = (t["uw"][:, :GDN_DV] - _dot(t["uw"][:, GDN_DV:].astype(BF16), t["sb"])).astype(BF16)
    for d, h in chains:
        t = st[d, h]
        refs[d][4][:, t["sl"]] = t["qs"] + _dot((t["qk"] * t["decay"]).astype(BF16), t["vb"])
        s_ref[d, h] = t["s"] * jnp.exp(t["glast"]) + _dot_tn(t["kdec"], t["vb"])


def _gdn_scan_call(q, k, v, gbm, b, l, lc):
    r = q.shape[0]
    c = GDN_C
    ncc, ncl = lc // c, l // c
    base = b * l // c

    def fwd(bi, s):
        return jnp.where(s < ncc, base + bi * ncc + s, bi * ncl + s - ncc)

    def bwd(bi, s):
        return jnp.where(s < ncc, base + bi * ncc + (ncc - 1 - s), bi * ncl + (ncl - 1) - (s - ncc))

    def specs(fn):
        return [pl.BlockSpec((c, HALF), lambda bi, s: (fn(bi, s), 0))] * 3 + [
            pl.BlockSpec((c, 128), lambda bi, s: (fn(bi, s), 0))]

    out_sd = jax.ShapeDtypeStruct((r, HALF), F32)
    return pl.pallas_call(
        _gdn_scan_kernel,
        grid=(b, ncc + ncl),
        in_specs=specs(fwd) + specs(bwd),
        out_specs=[pl.BlockSpec((c, HALF), lambda bi, s: (fwd(bi, s), 0)),
                   pl.BlockSpec((c, HALF), lambda bi, s: (bwd(bi, s), 0))],
        out_shape=[out_sd, out_sd],
        scratch_shapes=[pltpu.VMEM((2, GDN_H, GDN_DK, GDN_DV), F32)],
        compiler_params=_cparams(("parallel", "arbitrary")),
        name="gdn_scan",
    )(q, k, v, gbm, q, k, v, gbm)


def _da_prep_kernel(q_ref, k_ref, v_ref, cos_ref, sin_ref, gq_ref, gk_ref, qs_ref, kp_ref, vp_ref):
    cs, sn = cos_ref[...], sin_ref[...]
    lane = lax.broadcasted_iota(jnp.int32, cs.shape, 1)
    lo64 = lane < DA_DH
    lo32 = (lane % DA_DH) < (DA_DH // 2)

    def norm_rope(t, g):
        sq = t * t
        ms0 = jnp.sum(jnp.where(lo64, sq, 0.0), axis=-1, keepdims=True)
        ms1 = jnp.sum(jnp.where(lo64, 0.0, sq), axis=-1, keepdims=True)
        inv = lax.rsqrt(jnp.where(lo64, ms0, ms1) * (1.0 / DA_DH) + EPS)
        t = t * inv * g
        swapped = jnp.where(lo32, pltpu.roll(t, 128 - DA_DH // 2, 1), pltpu.roll(t, DA_DH // 2, 1))
        return t * cs + swapped * sn

    for h in range(DA_H):
        sl = slice(h * 128, (h + 1) * 128)
        qh = norm_rope(q_ref[:, sl], gq_ref[...]) * (DA_DH ** -0.5 * LOG2E)
        qs_ref[h, 0] = jnp.where(lo64, qh, 0.0).astype(BF16)
        qs_ref[h, 1] = jnp.where(lo64, 0.0, qh).astype(BF16)
        kp_ref[h] = norm_rope(k_ref[:, sl], gk_ref[...]).astype(BF16)
        vp_ref[h] = v_ref[:, sl].T.astype(BF16)


def _da_prep_call(proj, cos_t, sin_t, gq, gk, bl, l, tp=256):
    r = proj.shape[0]
    ntl = l // tp

    def pos(i):
        return jnp.where(i * tp < bl, i % ntl, ntl)

    return pl.pallas_call(
        _da_prep_kernel,
        grid=(r // tp,),
        in_specs=[pl.BlockSpec((tp, HALF), lambda i: (i, COL_Q // HALF)),
                  pl.BlockSpec((tp, HALF), lambda i: (i, COL_K // HALF)),
                  pl.BlockSpec((tp, HALF), lambda i: (i, COL_V // HALF)),
                  pl.BlockSpec((tp, 128), lambda i: (pos(i), 0)),
                  pl.BlockSpec((tp, 128), lambda i: (pos(i), 0)),
                  pl.BlockSpec((1, 128), lambda i: (0, 0)),
                  pl.BlockSpec((1, 128), lambda i: (0, 0))],
        out_specs=[pl.BlockSpec((DA_H, 2, tp, 128), lambda i: (0, 0, i, 0)),
                   pl.BlockSpec((DA_H, tp, 128), lambda i: (0, i, 0)),
                   pl.BlockSpec((DA_H, DA_DV, tp), lambda i: (0, 0, i))],
        out_shape=[jax.ShapeDtypeStruct((DA_H, 2, r, 128), BF16),
                   jax.ShapeDtypeStruct((DA_H, r, 128), BF16),
                   jax.ShapeDtypeStruct((DA_H, DA_DV, r), BF16)],
        compiler_params=_cparams(("parallel",)),
        name="da_prep",
    )(proj, proj, proj, cos_t, sin_t, gq, gk)


def _attn_kernel(q_ref, kc_ref, vc_ref, kl_ref, vl_ref, lam_ref, sg_ref, o_ref, m_ref, l_ref, acc_ref,
                 s0_ref, s1_ref, p0_ref, p1_ref, cm_ref, al_ref, *, tq, tkc, n_lat_tiles, lam_init):
    q = q_ref[...].reshape(2 * tq, 128)

    st = _dot_nt(kc_ref[...], q)
    m0 = jnp.max(st, axis=0, keepdims=True)
    p = jnp.exp2(st - m0)
    m_ref[...] = m0
    l_ref[...] = jnp.sum(p, axis=0, keepdims=True)
    acc_ref[...] = _dot(vc_ref[...], p.astype(BF16))

    def scores(j, s_ref, slot):
        off = pl.multiple_of(j * tkc, tkc)
        st = _dot_nt(kl_ref[pl.ds(off, tkc), :], q)
        s_ref[...] = st
        cm_ref[slot:slot + 1, :] = jnp.max(st, axis=0, keepdims=True)

    def softmax(s_ref, p_ref, slot):
        m_old = m_ref[...]
        m_new = jnp.maximum(m_old, cm_ref[slot:slot + 1, :])
        alpha = jnp.exp2(m_old - m_new)
        p = jnp.exp2(s_ref[...] - m_new)
        l_ref[...] = alpha * l_ref[...] + jnp.sum(p, axis=0, keepdims=True)
        p_ref[...] = p.astype(BF16)
        al_ref[slot:slot + 1, :] = alpha
        m_ref[...] = m_new

    def weighted_values(j, p_ref, slot):
        off = pl.multiple_of(j * tkc, tkc)
        acc_ref[...] = al_ref[slot:slot + 1, :] * acc_ref[...] + _dot(vl_ref[:, pl.ds(off, tkc)], p_ref[...])

    @pl.when(pl.program_id(2) < n_lat_tiles)
    def _():
        n = kl_ref.shape[0] // tkc
        scores(0, s0_ref, 0)
        p1_ref[...] = jnp.zeros_like(p1_ref)
        al_ref[1:2, :] = jnp.ones((1, 2 * tq), F32)

        def body(jj, carry):
            j = 2 * jj
            scores(j + 1, s1_ref, 1)
            softmax(s0_ref, p0_ref, 0)
            weighted_values(jnp.maximum(j - 1, 0), p1_ref, 1)
            scores(jnp.minimum(j + 2, n - 1), s0_ref, 0)
            softmax(s1_ref, p1_ref, 1)
            weighted_values(j, p0_ref, 0)
            return carry

        lax.fori_loop(0, n // 2, body, 0)
        weighted_values(n - 1, p1_ref, 1)

    lp = lam_ref[...]
    lam = (jnp.exp(jnp.sum(lp[0:1] * lp[1:2], axis=-1, keepdims=True))
           - jnp.exp(jnp.sum(lp[2:3] * lp[3:4], axis=-1, keepdims=True)) + lam_init)
    ot = acc_ref[...] * (1.0 / l_ref[...])
    ot = ot[:, :tq] - lam * ot[:, tq:]
    ot = ot * lax.rsqrt(jnp.mean(ot * ot, axis=0, keepdims=True) + EPS) * sg_ref[...]
    o_ref[...] = (ot * (1.0 - lam_init)).T


def _attn_call(qs, kp, vpt, lam_p, subln_g, b, l, lc, with_ctx_queries, lam_init, tq=256, tkc=1024):
    r = kp.shape[1]
    assert tq == lc and l % (2 * tkc) == 0
    ntl = l // tq
    nq = ntl + (1 if with_ctx_queries else 0)
    r_out = r if with_ctx_queries else b * l

    def qrow(bi, qi):
        return jnp.where(qi < ntl, bi * ntl + qi, b * ntl + bi)

    kern = functools.partial(_attn_kernel, tq=tq, tkc=tkc, n_lat_tiles=ntl, lam_init=lam_init)
    return pl.pallas_call(
        kern,
        grid=(b, DA_H, nq),
        in_specs=[pl.BlockSpec((None, 2, tq, 128), lambda bi, h, qi: (h, 0, qrow(bi, qi), 0)),
                  pl.BlockSpec((None, lc, 128), lambda bi, h, qi: (h, b * l // lc + bi, 0)),
                  pl.BlockSpec((None, DA_DV, lc), lambda bi, h, qi: (h, 0, b * l // lc + bi)),
                  pl.BlockSpec((None, l, 128), lambda bi, h, qi: (h, bi, 0)),
                  pl.BlockSpec((None, DA_DV, l), lambda bi, h, qi: (h, 0, bi)),
                  pl.BlockSpec((4, DA_DH), lambda bi, h, qi: (0, 0)),
                  pl.BlockSpec((DA_DV, 1), lambda bi, h, qi: (0, 0))],
        out_specs=pl.BlockSpec((tq, 128), lambda bi, h, qi: (qrow(bi, qi), h)),
        out_shape=jax.ShapeDtypeStruct((r_out, HALF), F32),
        scratch_shapes=[pltpu.VMEM((1, 2 * tq), F32), pltpu.VMEM((1, 2 * tq), F32), pltpu.VMEM((DA_DV, 2 * tq), F32),
                        pltpu.VMEM((tkc, 2 * tq), F32), pltpu.VMEM((tkc, 2 * tq), F32),
                        pltpu.VMEM((tkc, 2 * tq), BF16), pltpu.VMEM((tkc, 2 * tq), BF16),
                        pltpu.VMEM((8, 2 * tq), F32), pltpu.VMEM((8, 2 * tq), F32)],
        compiler_params=_cparams(("parallel", "parallel", "arbitrary")),
        name="attn",
    )(qs, kp, vpt, kp, vpt, lam_p, subln_g)


def _merge_kernel(of_ref, ob_ref, z_ref, uv_ref, yc_ref, gates_ref, x_ref, mod_ref, gn_ref, vn_ref, ws_ref, bs_ref,
                  wa_ref, wb_ref, wc_ref, wo_ref, o_ref, yb_ref, *, tm):
    oa = of_ref[...] + ob_ref[...]
    z = z_ref[...]
    parts = []
    for h in range(GDN_H):
        oh = oa[:, h * GDN_DV:(h + 1) * GDN_DV]
        parts.append(oh * lax.rsqrt(jnp.mean(oh * oh, axis=-1, keepdims=True) + EPS) * gn_ref[...])
    ya = jnp.concatenate(parts, axis=1) * _silu(z)
    uv = _gelu(uv_ref[...])
    u, v = uv[:, :HALF], uv[:, HALF:]
    v = (v * lax.rsqrt(jnp.mean(v * v, axis=-1, keepdims=True) + EPS) * vn_ref[...]).astype(BF16)
    for ck in range(tm // CMLP_CHUNK):
        rs = slice(ck * CMLP_CHUNK, (ck + 1) * CMLP_CHUNK)
        for g in range(CMLP_G):
            cs = slice(g * CMLP_CH, (g + 1) * CMLP_CH)
            sv = _dot(ws_ref[g], v[rs, cs]) + bs_ref[:, g:g + 1]
            yb_ref[rs, cs] = u[rs, cs] * sv
    yb = yb_ref[...]
    gts = _sigmoid(gates_ref[...])
    y = (gts[:, 0:D] * _dot(ya.astype(BF16), wa_ref[...])
         + gts[:, D:2 * D] * _dot(yb.astype(BF16), wb_ref[...])
         + gts[:, 2 * D:3 * D] * _dot(yc_ref[...].astype(BF16), wc_ref[...]))
    mix = _dot(y.astype(BF16), wo_ref[...])
    o_ref[...] = x_ref[...] + mod_ref[:, 2 * D:3 * D] * mix


def _merge_call(o_f, o_b, proj, yc, xs, mods, gn, vn, ws, bs_t, wa, wb, wc, wo, grp, r_out, tm=256):
    kern = functools.partial(_merge_kernel, tm=tm)
    full = lambda shape: pl.BlockSpec(shape, lambda i: (0,) * len(shape))
    return pl.pallas_call(
        kern,
        grid=(r_out // tm,),
        in_specs=[pl.BlockSpec((tm, HALF), lambda i: (i, 0)),
                  pl.BlockSpec((tm, HALF), lambda i: (i, 0)),
                  pl.BlockSpec((tm, HALF), lambda i: (i, COL_Z // HALF)),
                  pl.BlockSpec((tm, 2 * HALF), lambda i: (i, COL_UV // (2 * HALF))),
                  pl.BlockSpec((tm, HALF), lambda i: (i, 0)),
                  pl.BlockSpec((tm, 3 * D), lambda i: (i, 0)),
                  pl.BlockSpec((tm, D), lambda i: (i, 0)),
                  pl.BlockSpec((None, 1, 6 * D), lambda i: (grp(i * tm), 0, 0)),
                  full((1, GDN_DV)), full((1, HALF)), full((CMLP_G, CMLP_CHUNK, CMLP_CHUNK)),
                  full((CMLP_CHUNK, CMLP_G)),
                  full((HALF, D)), full((HALF, D)), full((HALF, D)), full((D, D))],
        out_specs=pl.BlockSpec((tm, D), lambda i: (i, 0)),
        out_shape=jax.ShapeDtypeStruct((r_out, D), F32),
        scratch_shapes=[pltpu.VMEM((tm, HALF), F32)],
        compiler_params=_cparams(("parallel",)),
        name="merge",
    )(o_f, o_b, proj, proj, yc, proj, xs, mods, gn, vn, ws, bs_t, wa, wb, wc, wo)


def _top_values(s, n):
    return _top_ranked(s, n)[0]


def _top_ranked(s, n):
    rid = lax.broadcasted_iota(jnp.int32, (n, s.shape[1]), 0)
    out = jnp.zeros((n, s.shape[1]), F32)
    rank = jnp.full(s.shape, float(n), F32)
    for k in range(n):
        m = jnp.max(s, axis=0, keepdims=True)
        out = jnp.where(rid == k, m, out)
        hit = s == m
        rank = jnp.where(hit, float(k), rank)
        s = jnp.where(hit, NEG, s)
    return out, rank


def _dup_bf16(x):
    bits = pltpu.bitcast(x, jnp.uint32)
    hi = (bits + jnp.uint32(0x7FFF) + ((bits >> 16) & jnp.uint32(1))) >> 16
    return pltpu.bitcast((hi << 16) | hi, F32)


def _peer_sel_kernel(x_ref, mod_ref, g_ref, wq_ref, sk_ref, hn_ref, n1_ref, r2_ref, e1_ref, e2_ref):
    x = x_ref[...]
    y = x * lax.rsqrt(jnp.mean(x * x, axis=-1, keepdims=True) + EPS) * g_ref[...]
    hn = (y * (1.0 + mod_ref[:, 4 * D:5 * D]) + mod_ref[:, 3 * D:4 * D]).astype(BF16)
    hn_ref[...] = hn
    q = _dot(hn, wq_ref[...]).astype(BF16)
    for h in range(PEER_H):
        half = PEER_QD // 2
        s1 = _dot_nt(sk_ref[h, 0], q[:, (2 * h) * half:(2 * h + 1) * half])
        s2 = _dot_nt(sk_ref[h, 1], q[:, (2 * h + 1) * half:(2 * h + 2) * half])
        v1 = _top_values(s1, PEER_K)
        v2, rank2 = _top_ranked(s2, PEER_K)
        rid16 = lax.broadcasted_iota(jnp.int32, v1.shape, 0)
        rid8 = lax.broadcasted_iota(jnp.int32, (8, v1.shape[1]), 0)
        cand = jnp.concatenate(
            [v1[0:1] + v2,
             v1[1:2] + v2[0:8], v1[2:3] + v2[0:8], v1[3:4] + v2[0:8],
             jnp.where(rid16 >= 4, v1 + v2[0:1], NEG),
             jnp.where(rid8 >= 4, v1[0:8] + v2[1:2], NEG),
             jnp.where(rid8 == 4, v1[0:8] + v2[2:3], NEG)], axis=0)
        best = _top_values(cand, PEER_K)
        zsum = jnp.sum(jnp.exp(best - best[0:1]), axis=0, keepdims=True)
        thr = best[PEER_K - 1:PEER_K]
        n1 = jnp.zeros_like(s1)
        for k in range(PEER_K):
            n1 = jnp.where(s1 + v2[k:k + 1] >= thr, float(k + 1), n1)
        n1_ref[h] = _dup_bf16(n1)
        r2_ref[h] = pltpu.bitcast(rank2.astype(BF16), F32)
        e1_ref[h] = _dup_bf16(jnp.exp(s1 - v1[0:1]) * (0.5 / zsum))
        e2_ref[h] = pltpu.bitcast(jnp.exp(s2 - v2[0:1]).astype(BF16), F32)


def _peer_sel_call(xs, mods, g, wq, sk, grp, r_act, ts=256):
    full = lambda shape: pl.BlockSpec(shape, lambda i: (0,) * len(shape))
    sc_spec = pl.BlockSpec((PEER_H, PEER_NK, ts), lambda i: (0, 0, i))
    pk_spec = pl.BlockSpec((PEER_H, PEER_NK // 2, ts), lambda i: (0, 0, i))
    sd32 = jax.ShapeDtypeStruct((PEER_H, PEER_NK, r_act), F32)
    sd16 = jax.ShapeDtypeStruct((PEER_H, PEER_NK // 2, r_act), F32)
    return pl.pallas_call(
        _peer_sel_kernel,
        grid=(r_act // ts,),
        in_specs=[pl.BlockSpec((ts, D), lambda i: (i, 0)),
                  pl.BlockSpec((None, 1, 6 * D), lambda i: (grp(i * ts), 0, 0)),
                  full((1, D)), full((D, PEER_H * PEER_QD)), full((PEER_H, 2, PEER_NK, PEER_QD // 2))],
        out_specs=[pl.BlockSpec((ts, D), lambda i: (i, 0)), sc_spec, pk_spec, sc_spec, pk_spec],
        out_shape=[jax.ShapeDtypeStruct((r_act, D), BF16), sd32, sd16, sd32, sd16],
        compiler_params=_cparams(("parallel",)),
        name="peer_sel",
    )(xs, mods, g, wq, sk)


def _peer_dense_kernel(hn_ref, ua_ref, ub_ref, va_ref, vb_ref, n1_ref, r2_ref, e1_ref, e2_ref, x_ref, mod_ref, o_ref,
                       acc_ref, g0_ref, g1_ref, g2_ref, w0_ref, w1_ref, *, te):
    s = pl.program_id(1)
    nr = te // PEER_NK
    tm = hn_ref.shape[0]
    tmh = tm // 2

    lanes = 128

    def rows16(row, ls):
        return pltpu.bitcast(jnp.broadcast_to(row[:, ls], (PEER_NK // 2, lanes)), BF16)

    def store_gates(et, g_ref):
        rows = [et * nr + r for r in range(nr)]
        e1s = [jnp.minimum(row, PEER_NK - 1) for row in rows]
        n_rows = [[n1_ref[h, pl.ds(e1s[r], 1), :] for r in range(nr)] for h in range(PEER_H)]
        w_rows = [[e1_ref[h, pl.ds(e1s[r], 1), :] for r in range(nr)] for h in range(PEER_H)]
        for c in range(tm // lanes):
            ls = slice(c * lanes, (c + 1) * lanes)
            gsum = [None] * nr
            for h in range(PEER_H):
                r2 = pltpu.bitcast(r2_ref[h, :, ls], BF16)
                e2 = pltpu.bitcast(e2_ref[h, :, ls], BF16)
                for r in range(nr):
                    term = jnp.where(r2 < rows16(n_rows[h][r], ls), rows16(w_rows[h][r], ls) * e2,
                                     jnp.zeros((), BF16))
                    gsum[r] = term if gsum[r] is None else gsum[r] + term
            for r in range(nr):
                g_ref[r, :, ls] = pltpu.bitcast(jnp.where(rows[r] < PEER_NK, gsum[r], jnp.zeros((), BF16)), F32)

    def store_weights(a, g_ref, w_ref):
        act = a + a * jnp.tanh(a * (0.7978845608028654 + (0.7978845608028654 * 0.044715) * (a * a)))
        for r in range(nr):
            rs = slice(r * PEER_NK, (r + 1) * PEER_NK)
            w_ref[rs, :] = pltpu.bitcast(g_ref[r], BF16) * act[rs].astype(BF16)

    @pl.when(s == 0)
    def _():
        acc_ref[...] = jnp.zeros_like(acc_ref)
        w1_ref[...] = jnp.zeros_like(w1_ref)
        store_gates(0, g0_ref)

    store_gates(2 * s + 1, g1_ref)
    store_gates(2 * s + 2, g2_ref)
    acc_ref[...] += _dot(va_ref[...], w1_ref[...])
    a0 = _dot_nt(ua_ref[...], hn_ref[...])
    store_weights(a0, g0_ref, w0_ref)
    a1 = _dot_nt(ub_ref[...], hn_ref[...])
    store_weights(a1, g1_ref, w1_ref)
    acc_ref[...] += _dot(vb_ref[...], w0_ref[...])
    g0_ref[...] = g2_ref[...]

    @pl.when(s == pl.num_programs(1) - 1)
    def _():
        o_ref[...] = x_ref[...] + mod_ref[:, 5 * D:6 * D] * acc_ref[...].T


def _peer_dense_call(hn, u_tab, vt_tab, n1, r2, e1, e2, xs, mods, grp, r_act, tm=512, te=512):
    ne = PEER_E // te
    assert ne % 2 == 0
    kern = functools.partial(_peer_dense_kernel, te=te)
    sc_spec = pl.BlockSpec((PEER_H, PEER_NK, tm), lambda i, s: (0, 0, i))
    pk_spec = pl.BlockSpec((PEER_H, PEER_NK // 2, tm), lambda i, s: (0, 0, i))
    last = ne - 1
    return pl.pallas_call(
        kern,
        grid=(r_act // tm, ne // 2 + 1),
        in_specs=[pl.BlockSpec((tm, D), lambda i, s: (i, 0)),
                  pl.BlockSpec((te, D), lambda i, s: (jnp.minimum(2 * s, last), 0)),
                  pl.BlockSpec((te, D), lambda i, s: (jnp.minimum(2 * s + 1, last), 0)),
                  pl.BlockSpec((D, te), lambda i, s: (0, jnp.maximum(2 * s - 1, 0))),
                  pl.BlockSpec((D, te), lambda i, s: (0, jnp.minimum(2 * s, last))),
                  sc_spec, pk_spec, sc_spec, pk_spec,
                  pl.BlockSpec((tm, D), lambda i, s: (i, 0)),
                  pl.BlockSpec((None, 1, 6 * D), lambda i, s: (grp(i * tm), 0, 0))],
        out_specs=pl.BlockSpec((tm, D), lambda i, s: (i, 0)),
        out_shape=jax.ShapeDtypeStruct((r_act, D), F32),
        scratch_shapes=[pltpu.VMEM((D, tm), F32),
                        pltpu.VMEM((te // PEER_NK, PEER_NK // 2, tm), F32),
                        pltpu.VMEM((te // PEER_NK, PEER_NK // 2, tm), F32),
                        pltpu.VMEM((te // PEER_NK, PEER_NK // 2, tm), F32),
                        pltpu.VMEM((te, tm), BF16), pltpu.VMEM((te, tm), BF16)],
        compiler_params=_cparams(("parallel", "arbitrary")),
        name="peer_dense",
    )(hn, u_tab, u_tab, vt_tab, vt_tab, n1, r2, e1, e2, xs, mods)


def _rope_tables(l, lc):
    rows = l // GRID_W
    r, col = jnp.meshgrid(jnp.arange(rows), jnp.arange(GRID_W), indexing='ij')
    n_freq = DA_DH // 4
    inv = ROPE_BASE ** (-jnp.arange(n_freq, dtype=F32) / n_freq)
    ang = jnp.concatenate([r.reshape(-1, 1).astype(F32) * inv, col.reshape(-1, 1).astype(F32) * inv], axis=-1)
    cos, sin = jnp.cos(ang), jnp.sin(ang)
    cos_t = jnp.concatenate([jnp.tile(cos, (1, 4)), jnp.ones((lc, 128), F32)], axis=0)
    sin_t = jnp.concatenate([jnp.tile(jnp.concatenate([-sin, sin], axis=1), (1, 2)), jnp.zeros((lc, 128), F32)], axis=0)
    return cos_t, sin_t


def kernel(x, c, ctx, c_ctx, w_mod, b_mod, norm1_g, w_in, gdn_conv_w, gdn_a_log, gdn_dt_bias, gdn_out_norm_g,
           cmlp_v_norm_g, cmlp_w_s, cmlp_b_s, da_q_norm_g, da_k_norm_g, da_lambda, da_subln_g, w_branch_a,
           w_branch_b, w_branch_c, w_out, norm2_g, peer_w_q, peer_subkeys, peer_u, peer_v):
    b, l, d = x.shape
    lc = ctx.shape[1]
    depth = w_mod.shape[0]
    bl = b * l
    r_all = bl + b * lc
    assert d == D and l % 512 == 0 and lc == 256 and bl % 512 == 0 and (b * lc) % 512 == 0

    def grp(row):
        return jnp.where(row < bl, row // l, b)

    xs = jnp.concatenate([x.reshape(bl, d), ctx.reshape(b * lc, d)], axis=0)
    cin = jnp.concatenate([c, c_ctx[None, :], jnp.zeros((8 - b - 1, d), F32)], axis=0)
    mods_all = _mod_call(cin, w_mod, b_mod)[:, :b + 1].reshape(depth, b + 1, 1, 6 * d)
    cos_t, sin_t = _rope_tables(l, lc)

    for li in range(depth):
        last = li == depth - 1
        lam_init = 0.8 - 0.6 * math.exp(-0.3 * li)
        mods = mods_all[li]
        w = w_in[li]
        w_re = jnp.concatenate(
            [w[:, 4624:7696], w[:, 0:1536], w[:, 1536:2048], w[:, 2064:3088], w[:, 3088:3600], w[:, 3600:4112],
             w[:, 4112:4624], w[:, 2048:2064], jnp.zeros((d, N_PROJ - COL_BA - 16), F32)], axis=1).astype(BF16)
        proj = _inproj_call(xs, mods, norm1_g[li][None, :], w_re, grp)

        gparams = jnp.zeros((2, 128), F32)
        gparams = gparams.at[0, 8:16].set(gdn_a_log[li].reshape(-1)).at[1, 8:16].set(gdn_dt_bias[li].reshape(-1))
        gq, gk, gv, gbm = _gdn_prep_call(proj, gdn_conv_w[li], gparams, bl, l, lc)
        o_f, o_b = _gdn_scan_call(gq, gk, gv, gbm, b, l, lc)

        gq2 = jnp.tile(da_q_norm_g[li], 2)[None, :]
        gk2 = jnp.tile(da_k_norm_g[li], 2)[None, :]
        qs, kp, vp = _da_prep_call(proj, cos_t, sin_t, gq2, gk2, bl, l)
        yc = _attn_call(qs, kp, vp, da_lambda[li], da_subln_g[li][:, None], b, l, lc, not last, lam_init)

        r_act = bl if last else r_all
        xs = _merge_call(o_f, o_b, proj, yc, xs, mods, gdn_out_norm_g[li][None, :], cmlp_v_norm_g[li][None, :],
                         cmlp_w_s[li].astype(BF16), cmlp_b_s[li].T, w_branch_a[li].astype(BF16),
                         w_branch_b[li].astype(BF16), w_branch_c[li].astype(BF16), w_out[li].astype(BF16), grp, r_act)

        hn, n1, r2, e1, e2 = _peer_sel_call(xs, mods, norm2_g[li][None, :], peer_w_q[li].astype(BF16),
                                            peer_subkeys[li].astype(BF16), grp, r_act)
        xs = _peer_dense_call(hn, peer_u[li].astype(BF16), peer_v[li].T.astype(BF16), n1, r2, e1, e2, xs, mods,
                              grp, r_act)
    return xs.reshape(b, l, d)
```
